```python
import math
import jax, jax.numpy as jnp
from jax import lax
import numpy as np

D_MODEL = 1024
BATCH = 1
SEQ = 16384
DEPTH = 1
DEC_BATCH = 32
DEC_SEQ = 8
PAST_LEN = 16384
PAGE_SIZE = 128

ML_HEADS = 4
ML_DQK = D_MODEL // 8
ML_DV = D_MODEL // 4
ML_CHUNK = 64
GATE_CAP = 15.0
MB_HEADS = 8
MB_DH = D_MODEL // 8
MB_BLOCK = 256
MB_TOPK = 3
MB_QCHUNK = 32
X_HEADS = 4
X_DH = D_MODEL // 4
N_MEM = 256
REL_BUCKETS = 32
REL_MAX_DIST = 1024
D_FF = ((8 * D_MODEL + 3 * 256 - 1) // (3 * 256)) * 256
EPS = 1e-6

kernel_name = 'hybrid_mlstm_moba_memxattn_decode_step'


def _split_sizes():
    return (ML_HEADS * ML_DQK, ML_HEADS * ML_DQK, ML_HEADS * ML_DV, ML_HEADS * ML_DV, ML_HEADS, ML_HEADS,
            MB_HEADS * MB_DH, MB_HEADS * MB_DH, MB_HEADS * MB_DH, X_HEADS * X_DH, 3 * D_MODEL)


def _rmsnorm(x, g):
    xf = x.astype(jnp.float32)
    y = xf * lax.rsqrt(jnp.mean(xf * xf, axis=-1, keepdims=True) + EPS)
    return (y * g.astype(jnp.float32)).astype(x.dtype)


def _softcap(x):
    return GATE_CAP * jnp.tanh(x / GATE_CAP)


def _rel_bucket(dist):
    max_exact = REL_BUCKETS // 2
    d = jnp.maximum(dist, 1).astype(jnp.float32)
    large = max_exact + (jnp.log(d / max_exact) / math.log(REL_MAX_DIST / max_exact)
                         * (REL_BUCKETS - max_exact)).astype(jnp.int32)
    large = jnp.minimum(large, REL_BUCKETS - 1)
    return jnp.where(dist < max_exact, dist, large)


def _mlstm_chunkwise(q, k, v, ig, lf, C0, n0, m0):
    B, T, H, dk = q.shape
    dv = v.shape[-1]
    L = ML_CHUNK if T % ML_CHUNK == 0 else T
    nc = T // L

    def chunks(a):
        a = a.astype(jnp.float32).reshape((B, nc, L, H) + a.shape[3:])
        return jnp.moveaxis(a, (1, 3), (0, 2))

    causal = jnp.tril(jnp.ones((L, L), dtype=bool))

    def step(carry, xs):
        C, n, m = carry
        qc, kc, vc, ic, fc = xs
        b = jnp.cumsum(fc, axis=-1)
        log_d = jnp.where(causal, b[..., :, None] - b[..., None, :] + ic[..., None, :], -jnp.inf)
        m_inter = b + m[..., None]
        m_t = jnp.maximum(m_inter, log_d.max(-1))
        w_intra = jnp.exp(log_d - m_t[..., None])
        w_inter = jnp.exp(m_inter - m_t)
        s = jnp.einsum('bhtd,bhsd->bhts', qc, kc) * w_intra
        num = w_inter[..., None] * jnp.einsum('bhtd,bhde->bhte', qc, C) + jnp.einsum('bhts,bhse->bhte', s, vc)
        den = w_inter * jnp.einsum('bhtd,bhd->bht', qc, n) + s.sum(-1)
        h = num / jnp.maximum(jnp.abs(den), jnp.exp(-m_t))[..., None]
        b_last = b[..., -1]
        log_w = b_last[..., None] - b + ic
        m_new = jnp.maximum(b_last + m, log_w.max(-1))
        w = jnp.exp(log_w - m_new[..., None])
        decay = jnp.exp(b_last + m - m_new)
        C = decay[..., None, None] * C + jnp.einsum('bhs,bhsd,bhse->bhde', w, kc, vc)
        n = decay[..., None] * n + jnp.einsum('bhs,bhsd->bhd', w, kc)
        return (C, n, m_new), h

    init = (C0.astype(jnp.float32), n0.astype(jnp.float32), m0.astype(jnp.float32))
    (C, n, m), hs = lax.scan(step, init, (chunks(q), chunks(k), chunks(v), chunks(ig), chunks(lf)))
    h = jnp.moveaxis(hs, (0, 2), (1, 3)).reshape(B, T, H, dv)
    return h, C, n, m


def _moba_attend(q, q_pos, means, fetch, rel_bias):
    B, T, H, D = q.shape
    nb = means.shape[1]
    n_sel = min(MB_TOPK, nb)
    N = B * T
    n_pad = -(-N // MB_QCHUNK) * MB_QCHUNK
    qf = jnp.pad(q.reshape(N, H, D), ((0, n_pad - N), (0, 0), (0, 0)))
    b_of = jnp.pad(jnp.repeat(jnp.arange(B, dtype=jnp.int32), T), (0, n_pad - N))
    p_of = jnp.pad(jnp.tile(q_pos.astype(jnp.int32), B), (0, n_pad - N))
    heads = jnp.arange(H, dtype=jnp.int32)[None, :, None, None]
    offs = jnp.arange(MB_BLOCK, dtype=jnp.int32)
    blocks = jnp.arange(nb, dtype=jnp.int32)
    scale = D ** -0.5

    def body(args):
        qc, bc, pc = args
        n_past = pc // MB_BLOCK
        gate = jnp.einsum('qhd,qnhd->qhn', qc.astype(jnp.float32), means[bc])
        gate = jnp.where(blocks[None, None, :] < n_past[:, None, None], gate, -1e30)
        _, sel = lax.top_k(gate, n_sel)
        own = jnp.broadcast_to(n_past[:, None, None], sel.shape[:2] + (1,))
        idx = jnp.concatenate([sel.astype(jnp.int32), own], axis=-1)
        valid = jnp.concatenate([sel < n_past[:, None, None], jnp.ones(own.shape, dtype=bool)], axis=-1)
        kpos = idx[..., None] * MB_BLOCK + offs
        mask = valid[..., None] & (kpos <= pc[:, None, None, None])
        kg, vg = fetch(bc[:, None, None, None], kpos, heads)
        dist = jnp.maximum(pc[:, None, None, None] - kpos, 0)
        bias = rel_bias[_rel_bucket(dist), heads]
        logits = jnp.einsum('qhd,qhjkd->qhjk', qc, kg).astype(jnp.float32) * scale + bias.astype(jnp.float32)
        logits = jnp.where(mask, logits, -1e30)
        probs = jax.nn.softmax(logits.reshape(logits.shape[:2] + (-1,)), axis=-1).reshape(logits.shape)
        return jnp.einsum('qhjk,qhjkd->qhd', probs.astype(vg.dtype), vg)

    nc = n_pad // MB_QCHUNK
    out = lax.map(body, (qf.reshape(nc, MB_QCHUNK, H, D), b_of.reshape(nc, MB_QCHUNK), p_of.reshape(nc, MB_QCHUNK)))
    return out.reshape(n_pad, H, D)[:N].reshape(B, T, H, D)


def _dense_source(k, v):
    B, S, H, D = k.shape
    nb = -(-S // MB_BLOCK)
    kp = jnp.pad(k.astype(jnp.float32), ((0, 0), (0, nb * MB_BLOCK - S), (0, 0), (0, 0)))
    means = kp.reshape(B, nb, MB_BLOCK, H, D).mean(axis=2)

    def fetch(bi, pos, hi):
        p = jnp.minimum(pos, S - 1)
        return k[bi, p, hi], v[bi, p, hi]
    return means, fetch


def _paged_source(cache_k, cache_v, page_table):
    def source(k_new, v_new):
        B, T, H, D = k_new.shape
        page = cache_k.shape[1]
        n_pages = page_table.shape[1]
        past = n_pages * page
        nb = -(-(past + T) // MB_BLOCK)
        page_sums = cache_k.astype(jnp.float32).sum(axis=1)[page_table]
        blk_page = (jnp.arange(n_pages, dtype=jnp.int32) * page) // MB_BLOCK
        blk_new = (past + jnp.arange(T, dtype=jnp.int32)) // MB_BLOCK
        sums = (jax.ops.segment_sum(jnp.swapaxes(page_sums, 0, 1), blk_page, num_segments=nb)
                + jax.ops.segment_sum(jnp.swapaxes(k_new.astype(jnp.float32), 0, 1), blk_new, num_segments=nb))
        means = jnp.swapaxes(sums, 0, 1) / MB_BLOCK

        def fetch(bi, pos, hi):
            lp = jnp.minimum(pos, past - 1)
            phys = page_table[bi, lp // page]
            row = lp % page
            tn = jnp.clip(pos - past, 0, T - 1)
            is_past = (pos < past)[..., None]
            kg = jnp.where(is_past, cache_k[phys, row, hi], k_new[bi, tn, hi])
            vg = jnp.where(is_past, cache_v[phys, row, hi], v_new[bi, tn, hi])
            return kg, vg
        return means, fetch
    return source


def _memory_kv(mem, mem_norm_g, w_mem_kv, cross_kn_g):
    B, M, _ = mem.shape
    kv = _rmsnorm(mem, mem_norm_g) @ w_mem_kv
    mk, mv = jnp.split(kv, 2, axis=-1)
    mk = _rmsnorm(mk.reshape(B, M, X_HEADS, X_DH), cross_kn_g)
    return mk, mv.reshape(B, M, X_HEADS, X_DH)


def _cross_attend(qx, mk, mv):
    logits = jnp.einsum('bthd,bmhd->bhtm', qx, mk).astype(jnp.float32) * (X_DH ** -0.5)
    probs = jax.nn.softmax(logits, axis=-1).astype(mv.dtype)
    return jnp.einsum('bhtm,bmhd->bthd', probs, mv)


def _layer(x, q_pos, ml_state, mem_k, mem_v, moba_source, params):
    (norm1_g, w_in, b_ig, b_fg, mlstm_hn_g, moba_qn_g, moba_kn_g, rel_bias,
     cross_qn_g, w_br_m, w_br_a, w_br_c, w_out, norm2_g, w_gu, w_down) = params
    B, T, _ = x.shape
    hin = _rmsnorm(x, norm1_g) @ w_in
    cuts = [int(c) for c in np.cumsum(_split_sizes())[:-1]]
    mq, mk, mv, mo, mi, mf, aq, ak, av, cq, gates = jnp.split(hin, cuts, axis=-1)
    mq = mq.reshape(B, T, ML_HEADS, ML_DQK)
    mk = mk.reshape(B, T, ML_HEADS, ML_DQK) * (ML_DQK ** -0.5)
    mv = mv.reshape(B, T, ML_HEADS, ML_DV)
    ig = _softcap(mi.astype(jnp.float32) + b_ig)
    lf = jax.nn.log_sigmoid(_softcap(mf.astype(jnp.float32) + b_fg))
    h, C, n, m = _mlstm_chunkwise(mq, mk, mv, ig, lf, ml_state[0], ml_state[1], ml_state[2])
    h = _rmsnorm(h, mlstm_hn_g.reshape(ML_HEADS, ML_DV)).reshape(B, T, ML_HEADS * ML_DV)
    y_m = jax.nn.sigmoid(mo) * h.astype(x.dtype)
    aq = _rmsnorm(aq.reshape(B, T, MB_HEADS, MB_DH), moba_qn_g)
    ak = _rmsnorm(ak.reshape(B, T, MB_HEADS, MB_DH), moba_kn_g)
    av = av.reshape(B, T, MB_HEADS, MB_DH)
    means, fetch = moba_source(ak, av)
    y_a = _moba_attend(aq, q_pos, means, fetch, rel_bias).reshape(B, T, MB_HEADS * MB_DH)
    cq = _rmsnorm(cq.reshape(B, T, X_HEADS, X_DH), cross_qn_g)
    y_c = _cross_attend(cq, mem_k, mem_v).reshape(B, T, X_HEADS * X_DH)
    g_m, g_a, g_c = jnp.split(jax.nn.sigmoid(gates), 3, axis=-1)
    mixed = g_m * (y_m @ w_br_m) + g_a * (y_a @ w_br_a) + g_c * (y_c.astype(x.dtype) @ w_br_c)
    x = x + mixed @ w_out
    gt, up = jnp.split(_rmsnorm(x, norm2_g) @ w_gu, 2, axis=-1)
    x = x + (jax.nn.silu(gt) * up) @ w_down
    return x, ak, av, C, n, m


def setup_inputs(seed: int = 0) -> dict:
    key = jax.random.key(seed)
    ks = jax.random.split(key, 32)
    n_pages = PAST_LEN // PAGE_SIZE
    n_pool = (DEC_BATCH * n_pages * 5) // 4
    n_in = sum(_split_sizes())

    def nrm(k, shape, s=1.0):
        return s * jax.random.normal(k, shape, jnp.float32)

    def gain(k, n):
        return 1.0 + 0.02 * jax.random.normal(k, (n,), jnp.float32)

    perm = jax.random.permutation(ks[11], n_pool)[: DEC_BATCH * n_pages]
    return {
        'x_prompt': nrm(ks[0], (BATCH, SEQ, D_MODEL)),
        'x_sample': nrm(ks[1], (DEC_BATCH, DEC_SEQ, D_MODEL)),
        'mem_prompt': nrm(ks[2], (BATCH, N_MEM, D_MODEL)),
        'cache_k': nrm(ks[3], (n_pool, PAGE_SIZE, MB_HEADS, MB_DH)),
        'cache_v': nrm(ks[4], (n_pool, PAGE_SIZE, MB_HEADS, MB_DH)),
        'cache_mem_k': nrm(ks[5], (DEC_BATCH, N_MEM, X_HEADS, X_DH)),
        'cache_mem_v': nrm(ks[6], (DEC_BATCH, N_MEM, X_HEADS, X_DH)),
        'state_C': nrm(ks[7], (DEC_BATCH, ML_HEADS, ML_DQK, ML_DV), 0.1),
        'state_n': nrm(ks[8], (DEC_BATCH, ML_HEADS, ML_DQK), 0.1),
        'state_m': nrm(ks[9], (DEC_BATCH, ML_HEADS)),
        'page_table': perm.reshape(DEC_BATCH, n_pages).astype(jnp.int32),
        'norm1_g': gain(ks[10], D_MODEL),
        'w_in': nrm(ks[12], (D_MODEL, n_in), D_MODEL ** -0.5),
        'b_ig': nrm(ks[13], (ML_HEADS,), 0.1),
        'b_fg': 3.0 + nrm(ks[14], (ML_HEADS,), 0.1),
        'mlstm_hn_g': gain(ks[15], ML_HEADS * ML_DV),
        'moba_qn_g': gain(ks[16], MB_DH),
        'moba_kn_g': gain(ks[17], MB_DH),
        'rel_bias': nrm(ks[18], (REL_BUCKETS, MB_HEADS), 0.5),
        'cross_qn_g': gain(ks[19], X_DH),
        'cross_kn_g': gain(ks[20], X_DH),
        'mem_norm_g': gain(ks[21], D_MODEL),
        'w_mem_kv': nrm(ks[22], (D_MODEL, 2 * X_HEADS * X_DH), D_MODEL ** -0.5),
        'w_br_m': nrm(ks[23], (ML_HEADS * ML_DV, D_MODEL), (ML_HEADS * ML_DV) ** -0.5),
        'w_br_a': nrm(ks[24], (MB_HEADS * MB_DH, D_MODEL), (MB_HEADS * MB_DH) ** -0.5),
        'w_br_c': nrm(ks[25], (X_HEADS * X_DH, D_MODEL), (X_HEADS * X_DH) ** -0.5),
        'w_out': nrm(ks[26], (D_MODEL, D_MODEL), D_MODEL ** -0.5),
        'norm2_g': gain(ks[27], D_MODEL),
        'w_gu': nrm(ks[28], (D_MODEL, 2 * D_FF), D_MODEL ** -0.5),
        'w_down': nrm(ks[29], (D_FF, D_MODEL), D_FF ** -0.5),
    }


def reference(x_prompt, x_sample, mem_prompt, cache_k, cache_v, cache_mem_k, cache_mem_v, state_C, state_n,
              state_m, page_table, norm1_g, w_in, b_ig, b_fg, mlstm_hn_g, moba_qn_g, moba_kn_g, rel_bias,
              cross_qn_g, cross_kn_g, mem_norm_g, w_mem_kv, w_br_m, w_br_a, w_br_c, w_out, norm2_g, w_gu, w_down):
    params = (norm1_g, w_in, b_ig, b_fg, mlstm_hn_g, moba_qn_g, moba_kn_g, rel_bias,
              cross_qn_g, w_br_m, w_br_a, w_br_c, w_out, norm2_g, w_gu, w_down)
    bp, sp = x_prompt.shape[0], x_prompt.shape[1]
    zero_state = (jnp.zeros((bp, ML_HEADS, ML_DQK, ML_DV), jnp.float32),
                  jnp.zeros((bp, ML_HEADS, ML_DQK), jnp.float32),
                  jnp.zeros((bp, ML_HEADS), jnp.float32))
    mem_k_p, mem_v_p = _memory_kv(mem_prompt, mem_norm_g, w_mem_kv, cross_kn_g)
    y_prompt, k_p, v_p, C_p, n_p, m_p = _layer(x_prompt, jnp.arange(sp, dtype=jnp.int32), zero_state,
                                               mem_k_p, mem_v_p, _dense_source, params)
    past = page_table.shape[1] * cache_k.shape[1]
    ts = x_sample.shape[1]
    y_sample, k_s, v_s, C_s, n_s, m_s = _layer(x_sample, past + jnp.arange(ts, dtype=jnp.int32),
                                               (state_C, state_n, state_m), cache_mem_k, cache_mem_v,
                                               _paged_source(cache_k, cache_v, page_table), params)
    return (y_prompt, y_sample, k_p, v_p, mem_k_p, mem_v_p, C_p, n_p, m_p, k_s, v_s, C_s, n_s, m_s)
```

```python
import functools
import math

import jax
import jax.numpy as jnp
from jax import lax
from jax.experimental import pallas as pl
from jax.experimental.pallas import tpu as pltpu

F32 = jnp.float32
BF16 = jnp.bfloat16

EPS = 1e-6
D_MODEL = 1024
ML_HEADS, ML_DQK, ML_DV = 4, 128, 256
GATE_CAP = 15.0
MB_HEADS, MB_DH, MB_BLOCK, MB_TOPK = 8, 128, 256, 3
X_HEADS, X_DH = 4, 256
REL_BUCKETS, REL_MAX_DIST = 32, 1024
NEG = -1e30
NEAR_BLOCKS = 5
N_SEG = 10

VMEM_LIMIT = 56 * 1024 * 1024


def _cparams(*sem):
    return pltpu.CompilerParams(dimension_semantics=sem, vmem_limit_bytes=VMEM_LIMIT)


def _resident(shape):
    nd = len(shape)
    return pl.BlockSpec(shape, lambda *_: (0,) * nd, pipeline_mode=pl.Buffered(1))


def _rms(x, g):
    return x * lax.rsqrt(jnp.mean(x * x, axis=-1, keepdims=True) + EPS) * g


def _head_rms(x, g, width):
    parts = [_rms(x[:, s:s + width], g) for s in range(0, x.shape[1], width)]
    return jnp.concatenate(parts, axis=1)


def _dot(a, b):
    return jnp.dot(a, b, preferred_element_type=F32)


def _dot_nt(a, b):
    return lax.dot_general(a, b, (((1,), (1,)), ((), ())), preferred_element_type=F32)


def _dot_tn(a, b):
    return lax.dot_general(a, b, (((0,), (0,)), ((), ())), preferred_element_type=F32)


def _split_hi_lo(x):
    hi = x.astype(BF16)
    lo = (x - hi.astype(F32)).astype(BF16)
    return hi, lo


def _rel_bucket(dist):
    max_exact = REL_BUCKETS // 2
    d = jnp.maximum(dist, 1).astype(F32)
    large = max_exact + (jnp.log(d / max_exact) / math.log(REL_MAX_DIST / max_exact)
                         * (REL_BUCKETS - max_exact)).astype(jnp.int32)
    large = jnp.minimum(large, REL_BUCKETS - 1)
    return jnp.where(dist < max_exact, dist, large)


def _inproj_kernel(x_ref, g1_ref, w_ref, wif_ref, bif_ref, qn_ref, kn_ref, cqn_ref,
                   qk_ref, v_ref, o_ref, aq_ref, ak32_ref, ak16_ref, av32_ref, av16_ref,
                   cq_ref, g_ref, gif_ref):
    xn = _rms(x_ref[...], g1_ref[...])
    xb = xn.astype(BF16)

    def seg(s):
        return _dot(xb, w_ref[:, s * 1024:(s + 1) * 1024])

    lane = lax.broadcasted_iota(jnp.int32, (1, 1024), 1)
    kscale = jnp.where(lane >= ML_HEADS * ML_DQK, ML_DQK ** -0.5, 1.0).astype(F32)
    qk_ref[...] = (seg(0) * kscale).astype(BF16)
    v_ref[...] = seg(1).astype(BF16)
    o_ref[...] = jax.nn.sigmoid(seg(2))
    aq_ref[...] = _head_rms(seg(3), qn_ref[...], MB_DH).astype(BF16)
    ak = _head_rms(seg(4), kn_ref[...], MB_DH)
    ak32_ref[...] = ak
    ak16_ref[...] = ak.astype(BF16)
    av = seg(5)
    av32_ref[...] = av
    av16_ref[...] = av.astype(BF16)
    cq_ref[...] = _head_rms(seg(6), cqn_ref[...], X_DH).astype(BF16)
    for s in range(3):
        g_ref[:, s * 1024:(s + 1) * 1024] = jax.nn.sigmoid(seg(7 + s))

    x_lo = (xn - xb.astype(F32)).astype(BF16)
    w_hi, w_lo = _split_hi_lo(wif_ref[...])
    z = _dot(xb, w_hi) + (_dot(x_lo, w_hi) + _dot(xb, w_lo)) + bif_ref[...]
    capped = GATE_CAP * jnp.tanh(z / GATE_CAP)
    log_sig = -(jnp.maximum(-capped, 0.0) + jnp.log1p(jnp.exp(-jnp.abs(capped))))
    col = lax.broadcasted_iota(jnp.int32, z.shape, 1)
    gif_ref[...] = jnp.where(col < ML_HEADS, capped, log_sig)


def _inproj(x, norm1_g, w_main, w_if, b_if, qn_g, kn_g, cqn_g, tm=256):
    n = x.shape[0]
    tm = min(tm, n)
    assert n % tm == 0
    row = lambda w: pl.BlockSpec((tm, w), lambda i: (i, 0))
    shapes = [(1024, BF16), (1024, BF16), (1024, F32), (1024, BF16), (1024, F32), (1024, BF16),
              (1024, F32), (1024, BF16), (1024, BF16), (3072, F32), (2 * ML_HEADS, F32)]
    return pl.pallas_call(
        _inproj_kernel,
        grid=(n // tm,),
        in_specs=[row(1024), _resident((1, 1024)), _resident(w_main.shape), _resident(w_if.shape),
                  _resident((1, 2 * ML_HEADS)), _resident((1, MB_DH)), _resident((1, MB_DH)),
                  _resident((1, X_DH))],
        out_specs=[row(w) for w, _ in shapes],
        out_shape=[jax.ShapeDtypeStruct((n, w), dt) for w, dt in shapes],
        compiler_params=_cparams("parallel"),
        name="inproj",
    )(x, norm1_g, w_main, w_if, b_if, qn_g, kn_g, cqn_g)


def _mlstm_kernel(qk_ref, v_ref, o_ref, gc_ref, gr_ref, c0_ref, n0_ref, m0_ref, hn_ref,
                  y_ref, cout_ref, nout_ref, mout_ref, c_s, n_s, m_s, *, chunk, n_chunks):
    L = chunk
    c = pl.program_id(1)

    @pl.when(c == 0)
    def _():
        c_s[...] = c0_ref[0]
        n_s[...] = n0_ref[0]
        m_s[...] = m0_ref[0]

    gc = gc_ref[0]
    gr = gr_ref[0]
    t_idx = lax.broadcasted_iota(jnp.int32, (L, L), 0)
    s_idx = lax.broadcasted_iota(jnp.int32, (L, L), 1)
    causal = s_idx <= t_idx

    for h in range(ML_HEADS):
        ig_row, lf_row = gr[h:h + 1, :], gr[ML_HEADS + h:ML_HEADS + h + 1, :]
        ig_col, lf_col = gc[:, h:h + 1], gc[:, ML_HEADS + h:ML_HEADS + h + 1]
        b_col = jnp.sum(jnp.where(causal, lf_row, 0.0), axis=1, keepdims=True)
        b_row = jnp.sum(jnp.where(t_idx <= s_idx, lf_col, 0.0), axis=0, keepdims=True)
        m_prev = m_s[:, h:h + 1]

        log_d = jnp.where(causal, b_col - b_row + ig_row, -jnp.inf)
        m_inter = b_col + m_prev
        m_t = jnp.maximum(m_inter, jnp.max(log_d, axis=1, keepdims=True))
        w_intra = jnp.exp(log_d - m_t)
        w_inter = jnp.exp(m_inter - m_t)

        q = qk_ref[0, :, h * ML_DQK:(h + 1) * ML_DQK]
        k = qk_ref[0, :, (ML_HEADS + h) * ML_DQK:(ML_HEADS + h + 1) * ML_DQK]
        v = v_ref[0, :, h * ML_DV:(h + 1) * ML_DV]
        c_h = c_s[h]
        n_h = n_s[h:h + 1, :]

        s = _dot_nt(q, k) * w_intra
        num = w_inter * _dot(q, c_h.astype(BF16)) + _dot(s.astype(BF16), v)
        qn = jnp.sum(q.astype(F32) * n_h, axis=1, keepdims=True)
        den = w_inter * qn + jnp.sum(s, axis=1, keepdims=True)
        hh = num / jnp.maximum(jnp.abs(den), jnp.exp(-m_t))

        b_last = b_col[L - 1:L, :]
        m_new = jnp.maximum(b_last + m_prev,
                            jnp.max(b_last - b_row + ig_row, axis=1, keepdims=True))
        w_col = jnp.exp(b_last - b_col + ig_col - m_new)
        decay = jnp.exp(b_last + m_prev - m_new)
        kw = k.astype(F32) * w_col
        c_s[h] = decay * c_h + _dot_tn(kw.astype(BF16), v)
        n_s[h:h + 1, :] = decay * n_h + jnp.sum(kw, axis=0, keepdims=True)
        m_s[:, h:h + 1] = m_new

        hn = _rms(hh, hn_ref[:, h * ML_DV:(h + 1) * ML_DV])
        y_ref[0, :, h * ML_DV:(h + 1) * ML_DV] = (o_ref[0, :, h * ML_DV:(h + 1) * ML_DV] * hn).astype(BF16)

    @pl.when(c == n_chunks - 1)
    def _():
        cout_ref[0] = c_s[...]
        nout_ref[0] = n_s[...]
        mout_ref[0] = m_s[...]


def _mlstm(qk, v, o_sig, gif, c0, n0, m0, hn_g, batch, chunk):
    n = qk.shape[0]
    t = n // batch
    nc = t // chunk
    chunked = lambda a: a.reshape(batch * nc, chunk, a.shape[-1])
    gc = chunked(gif)
    gr = gc.transpose(0, 2, 1)
    row = lambda w: pl.BlockSpec((1, chunk, w), lambda b, c: (b * nc + c, 0, 0))
    state = lambda shape: pl.BlockSpec((1,) + shape, lambda b, c: (b,) + (0,) * len(shape))
    y, c_out, n_out, m_out = pl.pallas_call(
        functools.partial(_mlstm_kernel, chunk=chunk, n_chunks=nc),
        grid=(batch, nc),
        in_specs=[row(1024), row(1024), row(1024), row(2 * ML_HEADS),
                  pl.BlockSpec((1, 2 * ML_HEADS, chunk), lambda b, c: (b * nc + c, 0, 0)),
                  state((ML_HEADS, ML_DQK, ML_DV)), state((ML_HEADS, ML_DQK)), state((1, ML_HEADS)),
                  _resident((1, ML_HEADS * ML_DV))],
        out_specs=[row(1024), state((ML_HEADS, ML_DQK, ML_DV)), state((ML_HEADS, ML_DQK)),
                   state((1, ML_HEADS))],
        out_shape=[jax.ShapeDtypeStruct((batch * nc, chunk, 1024), BF16),
                   jax.ShapeDtypeStruct((batch, ML_HEADS, ML_DQK, ML_DV), F32),
                   jax.ShapeDtypeStruct((batch, ML_HEADS, ML_DQK), F32),
                   jax.ShapeDtypeStruct((batch, 1, ML_HEADS), F32)],
        scratch_shapes=[pltpu.VMEM((ML_HEADS, ML_DQK, ML_DV), F32), pltpu.VMEM((ML_HEADS, ML_DQK), F32),
                        pltpu.VMEM((1, ML_HEADS), F32)],
        compiler_params=_cparams("parallel", "arbitrary"),
        name="mlstm",
    )(chunked(qk), chunked(v), chunked(o_sig), gc, gr, c0, n0, m0.reshape(batch, 1, ML_HEADS), hn_g)
    return y.reshape(n, 1024), c_out, n_out, m_out


def _bucket_lookup(bucket, table_row):
    val = jnp.zeros(bucket.shape, F32)
    for b in range(REL_BUCKETS):
        val = jnp.where(bucket == b, table_row(b), val)
    return val


def _prompt_bias_kernel(rb_ref, o_ref):
    h = pl.program_id(0)
    qi = lax.broadcasted_iota(jnp.int32, (MB_BLOCK, MB_BLOCK), 0)
    kj = lax.broadcasted_iota(jnp.int32, (MB_BLOCK, MB_BLOCK), 1)
    for d in range(NEAR_BLOCKS + 1):
        bucket = _rel_bucket(jnp.maximum(d * MB_BLOCK + qi - kj, 0))
        o_ref[0, d] = _bucket_lookup(bucket, lambda b: rb_ref[b, h])


def _prompt_bias(rel_bias):
    return pl.pallas_call(
        _prompt_bias_kernel,
        grid=(MB_HEADS,),
        in_specs=[pl.BlockSpec(memory_space=pltpu.SMEM)],
        out_specs=pl.BlockSpec((1, NEAR_BLOCKS + 1, MB_BLOCK, MB_BLOCK), lambda h: (h, 0, 0, 0)),
        out_shape=jax.ShapeDtypeStruct((MB_HEADS, NEAR_BLOCKS + 1, MB_BLOCK, MB_BLOCK), F32),
        compiler_params=_cparams("parallel"),
        name="prompt_bias",
    )(rel_bias)


def _sample_bias_kernel(rbl_ref, near_ref, new_ref, far_ref, *, t_new):
    n_near = near_ref.shape[0]
    lanes = near_ref.shape[1]
    lookup = lambda bucket: _bucket_lookup(bucket, lambda b: rbl_ref[b:b + 1, :])
    r = lax.broadcasted_iota(jnp.int32, (n_near, lanes), 0)
    tq = lax.broadcasted_iota(jnp.int32, (n_near, lanes), 1) % t_new
    near_ref[...] = lookup(_rel_bucket(jnp.maximum(tq + n_near - r, 0)))
    r = lax.broadcasted_iota(jnp.int32, (t_new, lanes), 0)
    tq = lax.broadcasted_iota(jnp.int32, (t_new, lanes), 1) % t_new
    new_ref[...] = lookup(_rel_bucket(jnp.maximum(tq - r, 0)))
    far = jnp.full((8, lanes), n_near + 1, jnp.int32)
    far_ref[...] = lookup(_rel_bucket(far))


def _sample_bias(rb_lanes, t_new):
    n_near = (NEAR_BLOCKS - 1) * MB_BLOCK
    lanes = rb_lanes.shape[1]
    return pl.pallas_call(
        functools.partial(_sample_bias_kernel, t_new=t_new),
        out_shape=[jax.ShapeDtypeStruct((n_near, lanes), F32), jax.ShapeDtypeStruct((t_new, lanes), F32),
                   jax.ShapeDtypeStruct((8, lanes), F32)],
        name="sample_bias",
    )(rb_lanes)


def _moba_prompt_kernel(q_ref, k_ref, v_ref, bias_ref, o_ref, means_s, *, n_blocks):
    i = pl.program_id(1)
    blk = lambda ref, j: ref[pl.ds(pl.multiple_of(j * MB_BLOCK, MB_BLOCK), MB_BLOCK), :]

    @pl.when(i == 0)
    def _():
        def body(j, carry):
            kb = blk(k_ref, j).astype(F32)
            means_s[pl.ds(j, 1), :] = jnp.sum(kb, axis=0, keepdims=True) * (1.0 / MB_BLOCK)
            return carry
        lax.fori_loop(0, n_blocks, body, 0)

    q = q_ref[...]
    scale = MB_DH ** -0.5
    m_hi, m_lo = _split_hi_lo(means_s[...])
    gate = _dot_nt(q, m_hi) + _dot_nt(q, m_lo)
    col = lax.broadcasted_iota(jnp.int32, gate.shape, 1)
    gate = jnp.where(col < i, gate, NEG)
    sel = []
    for _ in range(MB_TOPK):
        mx = jnp.max(gate, axis=1, keepdims=True)
        idx = jnp.min(jnp.where(gate == mx, col, n_blocks), axis=1, keepdims=True)
        sel.append(jnp.where(idx < i, idx, -1))
        gate = jnp.where(col == idx, -jnp.inf, gate)

    qi = lax.broadcasted_iota(jnp.int32, (MB_BLOCK, MB_BLOCK), 0)
    kj = lax.broadcasted_iota(jnp.int32, (MB_BLOCK, MB_BLOCK), 1)
    s = jnp.where(kj <= qi, _dot_nt(q, blk(k_ref, i)) * scale + bias_ref[0, 0], NEG)
    m0 = jnp.max(s, axis=1, keepdims=True)
    p = jnp.exp(s - m0)
    l0 = jnp.sum(p, axis=1, keepdims=True)
    acc0 = _dot(p.astype(BF16), blk(v_ref, i))

    def body(j, carry):
        m, l, acc = carry
        chosen = (sel[0] == j) | (sel[1] == j) | (sel[2] == j)
        bias = bias_ref[0, jnp.minimum(i - j, NEAR_BLOCKS)]
        s = jnp.where(chosen, _dot_nt(q, blk(k_ref, j)) * scale + bias, NEG)
        m_new = jnp.maximum(m, jnp.max(s, axis=1, keepdims=True))
        alpha = jnp.exp(m - m_new)
        p = jnp.exp(s - m_new)
        l = alpha * l + jnp.sum(p, axis=1, keepdims=True)
        acc = alpha * acc + _dot(p.astype(BF16), blk(v_ref, j))
        return m_new, l, acc

    _, l, acc = lax.fori_loop(0, i, body, (m0, l0, acc0))
    o_ref[...] = (acc / l).astype(BF16)


def _moba_prompt(aq, ak, av, bias_tiles):
    n = aq.shape[0]
    assert n % MB_BLOCK == 0
    nb = n // MB_BLOCK
    return pl.pallas_call(
        functools.partial(_moba_prompt_kernel, n_blocks=nb),
        grid=(MB_HEADS, nb),
        in_specs=[pl.BlockSpec((MB_BLOCK, MB_DH), lambda h, i: (i, h)),
                  pl.BlockSpec((n, MB_DH), lambda h, i: (0, h)),
                  pl.BlockSpec((n, MB_DH), lambda h, i: (0, h)),
                  pl.BlockSpec((1, NEAR_BLOCKS + 1, MB_BLOCK, MB_BLOCK), lambda h, i: (h, 0, 0, 0))],
        out_specs=pl.BlockSpec((MB_BLOCK, MB_DH), lambda h, i: (i, h)),
        out_shape=jax.ShapeDtypeStruct((n, MB_HEADS * MB_DH), BF16),
        scratch_shapes=[pltpu.VMEM((nb, MB_DH), F32)],
        compiler_params=_cparams("parallel", "arbitrary"),
        name="moba_prompt",
    )(aq, ak, av, bias_tiles)


PAGES_PER_STEP = 8


def _moba_sample_scores_kernel(pt_ref, qbd_ref, *refs, page, past, t_new):
    k_refs = refs[:PAGES_PER_STEP]
    knew_ref, near_ref, newb_ref, far_ref, p_ref, pnew_ref, l_ref, s_s, gate_s, sel_s = refs[PAGES_PER_STEP:]
    del pt_ref
    step = pl.program_id(1)
    n_steps = pl.num_programs(1)
    nbp = past // MB_BLOCK
    n_far = nbp - (NEAR_BLOCKS - 1)
    lanes = MB_HEADS * t_new
    scale = MB_DH ** -0.5
    qbd = qbd_ref[0]

    for r in range(PAGES_PER_STEP):
        kp = k_refs[r][0].astype(BF16)
        off = pl.multiple_of((step * PAGES_PER_STEP + r) * page, page)
        s_s[pl.ds(off, page), :] = _dot(kp, qbd)

    @pl.when(step == n_steps - 1)
    def _():
        rows = lambda j: pl.ds(pl.multiple_of(j * MB_BLOCK, MB_BLOCK), MB_BLOCK)

        def gate_body(j, carry):
            gate_s[pl.ds(j, 1), :] = jnp.sum(s_s[rows(j), :], axis=0, keepdims=True) * (1.0 / MB_BLOCK)
            return carry
        lax.fori_loop(0, nbp, gate_body, 0)

        gate = gate_s[...]
        bidx = lax.broadcasted_iota(jnp.int32, gate.shape, 0)
        chosen = jnp.zeros(gate.shape, F32)
        for _ in range(MB_TOPK):
            mx = jnp.max(gate, axis=0, keepdims=True)
            idx = jnp.min(jnp.where(gate == mx, bidx, nbp), axis=0, keepdims=True)
            hit = (bidx == idx) & (mx > -jnp.inf)
            chosen = jnp.where(hit, 1.0, chosen)
            gate = jnp.where(bidx == idx, -jnp.inf, gate)
        sel_s[...] = chosen

        far_bias = far_ref[0:1, :]

        def logits(j, bias):
            return jnp.where(sel_s[pl.ds(j, 1), :] > 0.0, s_s[rows(j), :] * scale + bias, NEG)

        def near_bias(j):
            return near_ref[(j - n_far) * MB_BLOCK:(j - n_far + 1) * MB_BLOCK, :]

        tk = lax.broadcasted_iota(jnp.int32, (t_new, lanes), 0)
        tq = lax.broadcasted_iota(jnp.int32, (t_new, lanes), 1) % t_new
        s_new = jnp.where(tk <= tq, _dot(knew_ref[0], qbd) * scale + newb_ref[...], NEG)

        m = jnp.max(s_new, axis=0, keepdims=True)
        m = lax.fori_loop(0, n_far, lambda j, m: jnp.maximum(
            m, jnp.max(logits(j, far_bias), axis=0, keepdims=True)), m)
        for j in range(n_far, nbp):
            m = jnp.maximum(m, jnp.max(logits(j, near_bias(j)), axis=0, keepdims=True))

        e_new = jnp.exp(s_new - m)
        pnew_ref[0] = e_new
        l = jnp.sum(e_new, axis=0, keepdims=True)

        def exp_block(j, bias, l):
            e = jnp.exp(logits(j, bias) - m)
            p_ref[0, rows(j), :] = e.astype(BF16)
            return l + jnp.sum(e, axis=0, keepdims=True)

        l = lax.fori_loop(0, n_far, lambda j, l: exp_block(j, far_bias, l), l)
        for j in range(n_far, nbp):
            l = exp_block(j, near_bias(j), l)
        l_ref[0] = l


def _moba_sample_values_kernel(pt_ref, p_ref, *refs, page, t_new):
    v_refs = refs[:PAGES_PER_STEP]
    pnew_ref, vnew_ref, l_ref, o_ref, acc_s = refs[PAGES_PER_STEP:]
    del pt_ref
    step = pl.program_id(1)

    @pl.when(step == 0)
    def _():
        acc_s[...] = jnp.zeros_like(acc_s)

    for r in range(PAGES_PER_STEP):
        vp = v_refs[r][0].astype(BF16)
        acc_s[...] += _dot_tn(p_ref[0, r * page:(r + 1) * page, :], vp)

    @pl.when(step == pl.num_programs(1) - 1)
    def _():
        acc = acc_s[...] + _dot_tn(pnew_ref[0].astype(BF16), vnew_ref[0])
        acc = acc / l_ref[0]
        parts = [acc[h * t_new:(h + 1) * t_new, h * MB_DH:(h + 1) * MB_DH] for h in range(MB_HEADS)]
        o_ref[0] = jnp.concatenate(parts, axis=1).astype(BF16)


def _moba_sample(aq, ak16, av16, cache_k, cache_v, page_table, near_b, new_b, far_b, batch):
    t_new = aq.shape[0] // batch
    n_pool, page = cache_k.shape[:2]
    n_pages = page_table.shape[1]
    past = n_pages * page
    lanes = MB_HEADS * t_new
    assert n_pages % PAGES_PER_STEP == 0 and past % MB_BLOCK == 0 and past // MB_BLOCK >= NEAR_BLOCKS - 1
    n_steps = n_pages // PAGES_PER_STEP
    ck = cache_k.reshape(n_pool, page, MB_HEADS * MB_DH)
    cv = cache_v.reshape(n_pool, page, MB_HEADS * MB_DH)
    q4 = aq.reshape(batch, t_new, MB_HEADS, MB_DH)
    qbd = jnp.einsum('bthd,hg->bhdgt', q4, jnp.eye(MB_HEADS, dtype=aq.dtype)).reshape(
        batch, MB_HEADS * MB_DH, lanes)

    def page_spec(r):
        return pl.BlockSpec((1, page, MB_HEADS * MB_DH),
                            lambda b, s, pt: (pt[b, s * PAGES_PER_STEP + r], 0, 0))

    per_batch = lambda shape: pl.BlockSpec((1,) + shape, lambda b, s, pt: (b,) + (0,) * len(shape))
    const = lambda shape: pl.BlockSpec(shape, lambda b, s, pt: (0,) * len(shape))

    p, p_new, l = pl.pallas_call(
        functools.partial(_moba_sample_scores_kernel, page=page, past=past, t_new=t_new),
        grid_spec=pltpu.PrefetchScalarGridSpec(
            num_scalar_prefetch=1,
            grid=(batch, n_steps),
            in_specs=[per_batch((MB_HEADS * MB_DH, lanes))] + [page_spec(r) for r in range(PAGES_PER_STEP)]
                     + [per_batch((t_new, MB_HEADS * MB_DH)), const(near_b.shape), const(new_b.shape),
                        const(far_b.shape)],
            out_specs=[per_batch((past, lanes)), per_batch((t_new, lanes)), per_batch((1, lanes))],
            scratch_shapes=[pltpu.VMEM((past, lanes), F32), pltpu.VMEM((past // MB_BLOCK, lanes), F32),
                            pltpu.VMEM((past // MB_BLOCK, lanes), F32)]),
        out_shape=[jax.ShapeDtypeStruct((batch, past, lanes), BF16),
                   jax.ShapeDtypeStruct((batch, t_new, lanes), F32),
                   jax.ShapeDtypeStruct((batch, 1, lanes), F32)],
        compiler_params=_cparams("parallel", "arbitrary"),
        name="moba_sample_scores",
    )(page_table, qbd, *([ck] * PAGES_PER_STEP), ak16.reshape(batch, t_new, -1), near_b, new_b, far_b)

    y = pl.pallas_call(
        functools.partial(_moba_sample_values_kernel, page=page, t_new=t_new),
        grid_spec=pltpu.PrefetchScalarGridSpec(
            num_scalar_prefetch=1,
            grid=(batch, n_steps),
            in_specs=[pl.BlockSpec((1, PAGES_PER_STEP * page, lanes), lambda b, s, pt: (b, s, 0))]
                     + [page_spec(r) for r in range(PAGES_PER_STEP)]
                     + [per_batch((t_new, lanes)), per_batch((t_new, MB_HEADS * MB_DH)), per_batch((lanes, 1))],
            out_specs=per_batch((t_new, MB_HEADS * MB_DH)),
            scratch_shapes=[pltpu.VMEM((lanes, MB_HEADS * MB_DH), F32)]),
        out_shape=jax.ShapeDtypeStruct((batch, t_new, MB_HEADS * MB_DH), BF16),
        compiler_params=_cparams("parallel", "arbitrary"),
        name="moba_sample_values",
    )(page_table, p, *([cv] * PAGES_PER_STEP), p_new, av16.reshape(batch, t_new, -1),
      l.reshape(batch, lanes, 1))
    return y.reshape(batch * t_new, MB_HEADS * MB_DH)


def _memkv_kernel(mem_ref, g_ref, w_ref, kn_ref, mk_ref, mv_ref):
    xb = _rms(mem_ref[...], g_ref[...]).astype(BF16)
    width = X_HEADS * X_DH
    mk_ref[...] = _head_rms(_dot(xb, w_ref[:, :width]), kn_ref[...], X_DH)
    mv_ref[...] = _dot(xb, w_ref[:, width:])


def _memkv(mem, mem_norm_g, w_mem_kv, cross_kn_g):
    m = mem.shape[0]
    width = X_HEADS * X_DH
    return pl.pallas_call(
        _memkv_kernel,
        out_shape=[jax.ShapeDtypeStruct((m, width), F32), jax.ShapeDtypeStruct((m, width), F32)],
        compiler_params=pltpu.CompilerParams(vmem_limit_bytes=VMEM_LIMIT),
        name="memkv",
    )(mem, mem_norm_g, w_mem_kv, cross_kn_g)


def _cross_kernel(q_ref, mk_ref, mv_ref, o_ref):
    scale = X_DH ** -0.5
    for h in range(X_HEADS):
        cols = slice(h * X_DH, (h + 1) * X_DH)
        s = _dot_nt(q_ref[0, :, cols], mk_ref[0, :, cols].astype(BF16)) * scale
        p = jnp.exp(s - jnp.max(s, axis=1, keepdims=True))
        p = p / jnp.sum(p, axis=1, keepdims=True)
        o_ref[0, :, cols] = _dot(p.astype(BF16), mv_ref[0, :, cols].astype(BF16)).astype(BF16)


def _cross(cq, mk, mv, batch, tq):
    n = cq.shape[0]
    t = n // batch
    assert t % tq == 0
    nt = t // tq
    width = X_HEADS * X_DH
    n_mem = mk.shape[1]
    q_spec = pl.BlockSpec((1, tq, width), lambda b, i: (b * nt + i, 0, 0))
    mem_spec = pl.BlockSpec((1, n_mem, width), lambda b, i: (b, 0, 0))
    y = pl.pallas_call(
        _cross_kernel,
        grid=(batch, nt),
        in_specs=[q_spec, mem_spec, mem_spec],
        out_specs=q_spec,
        out_shape=jax.ShapeDtypeStruct((batch * nt, tq, width), BF16),
        compiler_params=_cparams("parallel", "arbitrary"),
        name="cross",
    )(cq.reshape(batch * nt, tq, width), mk, mv)
    return y.reshape(n, width)


def _merge_kernel(x_ref, ym_ref, ya_ref, yc_ref, g_ref, wm_ref, wa_ref, wc_ref, wo_ref, o_ref):
    mixed = (g_ref[:, 0:1024] * _dot(ym_ref[...], wm_ref[...])
             + g_ref[:, 1024:2048] * _dot(ya_ref[...], wa_ref[...])
             + g_ref[:, 2048:3072] * _dot(yc_ref[...], wc_ref[...]))
    o_ref[...] = x_ref[...] + _dot(mixed.astype(BF16), wo_ref[...])


def _merge(x, y_m, y_a, y_c, g, w_m, w_a, w_c, w_o, tm=256):
    n = x.shape[0]
    tm = min(tm, n)
    assert n % tm == 0
    row = lambda w: pl.BlockSpec((tm, w), lambda i: (i, 0))
    return pl.pallas_call(
        _merge_kernel,
        grid=(n // tm,),
        in_specs=[row(1024), row(1024), row(1024), row(1024), row(3072)]
                 + [_resident((1024, 1024))] * 4,
        out_specs=row(1024),
        out_shape=jax.ShapeDtypeStruct((n, 1024), F32),
        compiler_params=_cparams("parallel"),
        name="merge",
    )(x, y_m, y_a, y_c, g, w_m, w_a, w_c, w_o)


def _swiglu_kernel(x_ref, g_ref, wgu_ref, wd_ref, o_ref):
    x = x_ref[...]
    gu = _dot(_rms(x, g_ref[...]).astype(BF16), wgu_ref[...])
    d_ff = wd_ref.shape[0]
    gt, up = gu[:, :d_ff], gu[:, d_ff:]
    o_ref[...] = x + _dot((gt * jax.nn.sigmoid(gt) * up).astype(BF16), wd_ref[...])


def _swiglu(x, norm2_g, w_gu, w_down, tm=256):
    n = x.shape[0]
    tm = min(tm, n)
    assert n % tm == 0
    row = pl.BlockSpec((tm, 1024), lambda i: (i, 0))
    return pl.pallas_call(
        _swiglu_kernel,
        grid=(n // tm,),
        in_specs=[row, _resident((1, 1024)), _resident(w_gu.shape), _resident(w_down.shape)],
        out_specs=row,
        out_shape=jax.ShapeDtypeStruct((n, 1024), F32),
        compiler_params=_cparams("parallel"),
        name="swiglu",
    )(x, norm2_g, w_gu, w_down)


def _pick_chunk(t):
    for c in (256, 128, 64):
        if t % c == 0:
            return c
    return t


def _layer(x3, ml_state, mem_k, mem_v, moba, w):
    batch, t, _ = x3.shape
    x = x3.reshape(batch * t, D_MODEL)
    (qk, v, o_sig, aq, ak32, ak16, av32, av16, cq, g, gif) = _inproj(
        x, w['norm1_g'], w['w_main'], w['w_if'], w['b_if'], w['moba_qn_g'], w['moba_kn_g'], w['cross_qn_g'])
    y_m, c_out, n_out, m_out = _mlstm(qk, v, o_sig, gif, ml_state[0], ml_state[1], ml_state[2],
                                      w['mlstm_hn_g'], batch, _pick_chunk(t))
    y_a = moba(aq, ak16, av16)
    y_c = _cross(cq, mem_k, mem_v, batch, min(t, 256))
    x1 = _merge(x, y_m, y_a, y_c, g, w['w_br_m'], w['w_br_a'], w['w_br_c'], w['w_out'])
    y = _swiglu(x1, w['norm2_g'], w['w_gu'], w['w_down'])
    return (y.reshape(batch, t, D_MODEL), ak32.reshape(batch, t, MB_HEADS, MB_DH),
            av32.reshape(batch, t, MB_HEADS, MB_DH), c_out, n_out, m_out.reshape(batch, ML_HEADS))


def kernel(x_prompt, x_sample, mem_prompt, cache_k, cache_v, cache_mem_k, cache_mem_v, state_C, state_n, state_m, page_table, norm1_g, w_in, b_ig, b_fg, mlstm_hn_g, moba_qn_g, moba_kn_g, rel_bias, cross_qn_g, cross_kn_g, mem_norm_g, w_mem_kv, w_br_m, w_br_a, w_br_c, w_out, norm2_g, w_gu, w_down):
    c_if = 2 * ML_HEADS * ML_DQK + 2 * ML_HEADS * ML_DV
    row = lambda a: a.reshape(1, -1)
    w = {
        'norm1_g': row(norm1_g),
        'w_main': jnp.concatenate([w_in[:, :c_if], w_in[:, c_if + 2 * ML_HEADS:]], axis=1).astype(BF16),
        'w_if': w_in[:, c_if:c_if + 2 * ML_HEADS],
        'b_if': row(jnp.concatenate([b_ig, b_fg])),
        'mlstm_hn_g': row(mlstm_hn_g), 'moba_qn_g': row(moba_qn_g), 'moba_kn_g': row(moba_kn_g),
        'cross_qn_g': row(cross_qn_g),
        'w_br_m': w_br_m.astype(BF16), 'w_br_a': w_br_a.astype(BF16), 'w_br_c': w_br_c.astype(BF16),
        'w_out': w_out.astype(BF16), 'norm2_g': row(norm2_g),
        'w_gu': w_gu.astype(BF16), 'w_down': w_down.astype(BF16),
    }
    assert w['w_main'].shape[1] == N_SEG * 1024

    bp, sp, _ = x_prompt.shape
    assert bp == 1
    zero_state = (jnp.zeros((bp, ML_HEADS, ML_DQK, ML_DV), F32), jnp.zeros((bp, ML_HEADS, ML_DQK), F32),
                  jnp.zeros((bp, ML_HEADS), F32))
    n_mem = mem_prompt.shape[1]
    mem_k_p, mem_v_p = _memkv(mem_prompt.reshape(bp * n_mem, D_MODEL), row(mem_norm_g),
                              w_mem_kv.astype(BF16), row(cross_kn_g))
    mem_k_p = mem_k_p.reshape(bp, n_mem, X_HEADS * X_DH)
    mem_v_p = mem_v_p.reshape(bp, n_mem, X_HEADS * X_DH)
    bias_tiles = _prompt_bias(rel_bias)
    y_p, k_p, v_p, c_p, n_p, m_p = _layer(
        x_prompt, zero_state, mem_k_p, mem_v_p,
        lambda aq, ak, av: _moba_prompt(aq, ak, av, bias_tiles), w)

    bs, ts, _ = x_sample.shape
    past = page_table.shape[1] * cache_k.shape[1]
    rb_lanes = jnp.repeat(rel_bias, ts, axis=1)
    near_b, new_b, far_b = _sample_bias(rb_lanes, ts)
    y_s, k_s, v_s, c_s, n_s, m_s = _layer(
        x_sample, (state_C, state_n, state_m),
        cache_mem_k.reshape(bs, n_mem, X_HEADS * X_DH), cache_mem_v.reshape(bs, n_mem, X_HEADS * X_DH),
        lambda aq, ak, av: _moba_sample(aq, ak, av, cache_k, cache_v, page_table, near_b, new_b, far_b, bs), w)

    return (y_p, y_s, k_p, v_p, mem_k_p.reshape(bp, n_mem, X_HEADS, X_DH),
            mem_v_p.reshape(bp, n_mem, X_HEADS, X_DH), c_p, n_p, m_p, k_s, v_s, c_s, n_s, m_s)
```

```python
import functools
import math

import jax
import jax.numpy as jnp
from jax import lax
from jax.experimental import pallas as pl
from jax.experimental.pallas import tpu as pltpu

F32 = jnp.float32
BF16 = jnp.bfloat16

EPS = 1e-6
D_MODEL = 1024
ML_HEADS, ML_DQK, ML_DV = 4, 128, 256
GATE_CAP = 15.0
MB_HEADS, MB_DH, MB_BLOCK, MB_TOPK = 8, 128, 256, 3
X_HEADS, X_DH = 4, 256
REL_BUCKETS, REL_MAX_DIST = 32, 1024
NEG = -1e30
LOG2E = math.log2(math.e)
NEAR_BLOCKS = 5
N_SEG = 10

VMEM_LIMIT = 56 * 1024 * 1024


def _cparams(*sem):
    return pltpu.CompilerParams(dimension_semantics=sem, vmem_limit_bytes=VMEM_LIMIT)


def _resident(shape):
    nd = len(shape)
    return pl.BlockSpec(shape, lambda *_: (0,) * nd, pipeline_mode=pl.Buffered(1))


def _rms(x, g):
    return x * lax.rsqrt(jnp.mean(x * x, axis=-1, keepdims=True) + EPS) * g


def _head_rms(x, g, width):
    parts = [_rms(x[:, s:s + width], g) for s in range(0, x.shape[1], width)]
    return jnp.concatenate(parts, axis=1)


def _dot(a, b):
    return jnp.dot(a, b, preferred_element_type=F32)


def _dot_nt(a, b):
    return lax.dot_general(a, b, (((1,), (1,)), ((), ())), preferred_element_type=F32)


def _dot_tn(a, b):
    return lax.dot_general(a, b, (((0,), (0,)), ((), ())), preferred_element_type=F32)


def _split_hi_lo(x):
    hi = x.astype(BF16)
    lo = (x - hi.astype(F32)).astype(BF16)
    return hi, lo


def _rel_bucket(dist):
    max_exact = REL_BUCKETS // 2
    d = jnp.maximum(dist, 1).astype(F32)
    large = max_exact + (jnp.log(d / max_exact) / math.log(REL_MAX_DIST / max_exact)
                         * (REL_BUCKETS - max_exact)).astype(jnp.int32)
    large = jnp.minimum(large, REL_BUCKETS - 1)
    return jnp.where(dist < max_exact, dist, large)


def _inproj_kernel(x_ref, g1_ref, w_ref, wif_ref, bif_ref, qn_ref, kn_ref, cqn_ref,
                   qk_ref, v_ref, o_ref, aq_ref, ak32_ref, ak16_ref, av32_ref, av16_ref, avt_ref,
                   cq_ref, g_ref, gif_ref):
    xn = _rms(x_ref[...], g1_ref[...])
    xb = xn.astype(BF16)

    def seg(s):
        return _dot(xb, w_ref[:, s * 1024:(s + 1) * 1024])

    lane = lax.broadcasted_iota(jnp.int32, (1, 1024), 1)
    kscale = jnp.where(lane >= ML_HEADS * ML_DQK, ML_DQK ** -0.5, 1.0).astype(F32)
    qk_ref[...] = (seg(0) * kscale).astype(BF16)
    v_ref[...] = seg(1).astype(BF16)
    o_ref[...] = jax.nn.sigmoid(seg(2))
    aq_ref[...] = _head_rms(seg(3), qn_ref[...], MB_DH).astype(BF16)
    ak = _head_rms(seg(4), kn_ref[...], MB_DH)
    ak32_ref[...] = ak
    ak16_ref[...] = ak.astype(BF16)
    av = seg(5)
    av32_ref[...] = av
    av16_ref[...] = av.astype(BF16)
    avt_ref[0] = av.T.astype(BF16)
    cq_ref[...] = _head_rms(seg(6), cqn_ref[...], X_DH).astype(BF16)
    for s in range(3):
        g_ref[:, s * 1024:(s + 1) * 1024] = jax.nn.sigmoid(seg(7 + s))

    x_lo = (xn - xb.astype(F32)).astype(BF16)
    w_hi, w_lo = _split_hi_lo(wif_ref[...])
    z = _dot(xb, w_hi) + (_dot(x_lo, w_hi) + _dot(xb, w_lo)) + bif_ref[...]
    capped = GATE_CAP * jnp.tanh(z / GATE_CAP)
    log_sig = -(jnp.maximum(-capped, 0.0) + jnp.log1p(jnp.exp(-jnp.abs(capped))))
    col = lax.broadcasted_iota(jnp.int32, z.shape, 1)
    gif_ref[...] = jnp.where(col < ML_HEADS, capped, log_sig)


def _inproj(x, norm1_g, w_main, w_if, b_if, qn_g, kn_g, cqn_g, tm=256):
    n = x.shape[0]
    tm = min(tm, n)
    assert n % tm == 0
    row = lambda w: pl.BlockSpec((tm, w), lambda i: (i, 0))
    shapes = [(1024, BF16), (1024, BF16), (1024, F32), (1024, BF16), (1024, F32), (1024, BF16),
              (1024, F32), (1024, BF16), None, (1024, BF16), (3072, F32), (2 * ML_HEADS, F32)]
    out_specs = [pl.BlockSpec((1, 1024, tm), lambda i: (i, 0, 0)) if s is None else row(s[0]) for s in shapes]
    out_shape = [jax.ShapeDtypeStruct((n // tm, 1024, tm), BF16) if s is None
                 else jax.ShapeDtypeStruct((n, s[0]), s[1]) for s in shapes]
    return pl.pallas_call(
        _inproj_kernel,
        grid=(n // tm,),
        in_specs=[row(1024), _resident((1, 1024)), _resident(w_main.shape), _resident(w_if.shape),
                  _resident((1, 2 * ML_HEADS)), _resident((1, MB_DH)), _resident((1, MB_DH)),
                  _resident((1, X_DH))],
        out_specs=out_specs,
        out_shape=out_shape,
        compiler_params=_cparams("parallel"),
        name="inproj",
    )(x, norm1_g, w_main, w_if, b_if, qn_g, kn_g, cqn_g)


def _mlstm_kernel(qk_ref, v_ref, o_ref, gc_ref, gr_ref, c0_ref, n0_ref, m0_ref, hn_ref,
                  y_ref, cout_ref, nout_ref, mout_ref, c_s, n_s, m_s, *, chunk, n_chunks):
    L = chunk
    c = pl.program_id(1)

    @pl.when(c == 0)
    def _():
        c_s[...] = c0_ref[0]
        n_s[...] = n0_ref[0]
        m_s[...] = m0_ref[0]

    gc = gc_ref[0]
    gr = gr_ref[0]
    t_idx = lax.broadcasted_iota(jnp.int32, (L, L), 0)
    s_idx = lax.broadcasted_iota(jnp.int32, (L, L), 1)
    causal = s_idx <= t_idx

    for h in range(ML_HEADS):
        ig_row, lf_row = gr[h:h + 1, :], gr[ML_HEADS + h:ML_HEADS + h + 1, :]
        ig_col, lf_col = gc[:, h:h + 1], gc[:, ML_HEADS + h:ML_HEADS + h + 1]
        b_col = jnp.sum(jnp.where(causal, lf_row, 0.0), axis=1, keepdims=True)
        b_row = jnp.sum(jnp.where(t_idx <= s_idx, lf_col, 0.0), axis=0, keepdims=True)
        m_prev = m_s[:, h:h + 1]

        log_d = jnp.where(causal, b_col - b_row + ig_row, -jnp.inf)
        m_inter = b_col + m_prev
        m_t = jnp.maximum(m_inter, jnp.max(log_d, axis=1, keepdims=True))
        w_intra = jnp.exp(log_d - m_t)
        w_inter = jnp.exp(m_inter - m_t)

        q = qk_ref[0, :, h * ML_DQK:(h + 1) * ML_DQK]
        k = qk_ref[0, :, (ML_HEADS + h) * ML_DQK:(ML_HEADS + h + 1) * ML_DQK]
        v = v_ref[0, :, h * ML_DV:(h + 1) * ML_DV]
        c_h = c_s[h]
        n_h = n_s[h:h + 1, :]

        s = _dot_nt(q, k) * w_intra
        num = w_inter * _dot(q, c_h.astype(BF16)) + _dot(s.astype(BF16), v)
        qn = jnp.sum(q.astype(F32) * n_h, axis=1, keepdims=True)
        den = w_inter * qn + jnp.sum(s, axis=1, keepdims=True)
        hh = num / jnp.maximum(jnp.abs(den), jnp.exp(-m_t))

        b_last = b_col[L - 1:L, :]
        m_new = jnp.maximum(b_last + m_prev,
                            jnp.max(b_last - b_row + ig_row, axis=1, keepdims=True))
        w_col = jnp.exp(b_last - b_col + ig_col - m_new)
        decay = jnp.exp(b_last + m_prev - m_new)
        kw = k.astype(F32) * w_col
        c_s[h] = decay * c_h + _dot_tn(kw.astype(BF16), v)
        n_s[h:h + 1, :] = decay * n_h + jnp.sum(kw, axis=0, keepdims=True)
        m_s[:, h:h + 1] = m_new

        hn = _rms(hh, hn_ref[:, h * ML_DV:(h + 1) * ML_DV])
        y_ref[0, :, h * ML_DV:(h + 1) * ML_DV] = (o_ref[0, :, h * ML_DV:(h + 1) * ML_DV] * hn).astype(BF16)

    @pl.when(c == n_chunks - 1)
    def _():
        cout_ref[0] = c_s[...]
        nout_ref[0] = n_s[...]
        mout_ref[0] = m_s[...]


def _mlstm(qk, v, o_sig, gif, c0, n0, m0, hn_g, batch, chunk):
    n = qk.shape[0]
    t = n // batch
    nc = t // chunk
    chunked = lambda a: a.reshape(batch * nc, chunk, a.shape[-1])
    gc = chunked(gif)
    gr = gc.transpose(0, 2, 1)
    row = lambda w: pl.BlockSpec((1, chunk, w), lambda b, c: (b * nc + c, 0, 0))
    state = lambda shape: pl.BlockSpec((1,) + shape, lambda b, c: (b,) + (0,) * len(shape))
    y, c_out, n_out, m_out = pl.pallas_call(
        functools.partial(_mlstm_kernel, chunk=chunk, n_chunks=nc),
        grid=(batch, nc),
        in_specs=[row(1024), row(1024), row(1024), row(2 * ML_HEADS),
                  pl.BlockSpec((1, 2 * ML_HEADS, chunk), lambda b, c: (b * nc + c, 0, 0)),
                  state((ML_HEADS, ML_DQK, ML_DV)), state((ML_HEADS, ML_DQK)), state((1, ML_HEADS)),
                  _resident((1, ML_HEADS * ML_DV))],
        out_specs=[row(1024), state((ML_HEADS, ML_DQK, ML_DV)), state((ML_HEADS, ML_DQK)),
                   state((1, ML_HEADS))],
        out_shape=[jax.ShapeDtypeStruct((batch * nc, chunk, 1024), BF16),
                   jax.ShapeDtypeStruct((batch, ML_HEADS, ML_DQK, ML_DV), F32),
                   jax.ShapeDtypeStruct((batch, ML_HEADS, ML_DQK), F32),
                   jax.ShapeDtypeStruct((batch, 1, ML_HEADS), F32)],
        scratch_shapes=[pltpu.VMEM((ML_HEADS, ML_DQK, ML_DV), F32), pltpu.VMEM((ML_HEADS, ML_DQK), F32),
                        pltpu.VMEM((1, ML_HEADS), F32)],
        compiler_params=_cparams("parallel", "arbitrary"),
        name="mlstm",
    )(chunked(qk), chunked(v), chunked(o_sig), gc, gr, c0, n0, m0.reshape(batch, 1, ML_HEADS), hn_g)
    return y.reshape(n, 1024), c_out, n_out, m_out


def _bucket_lookup(bucket, table_row):
    val = jnp.zeros(bucket.shape, F32)
    for b in range(REL_BUCKETS):
        val = jnp.where(bucket == b, table_row(b), val)
    return val


def _prompt_bias_kernel(rb_ref, o_ref):
    h = pl.program_id(0)
    kj = lax.broadcasted_iota(jnp.int32, (MB_BLOCK, MB_BLOCK), 0)
    qi = lax.broadcasted_iota(jnp.int32, (MB_BLOCK, MB_BLOCK), 1)
    for d in range(NEAR_BLOCKS + 1):
        bucket = _rel_bucket(jnp.maximum(d * MB_BLOCK + qi - kj, 0))
        o_ref[0, d] = _bucket_lookup(bucket, lambda b: rb_ref[b, h]) * LOG2E


def _prompt_bias(rel_bias):
    return pl.pallas_call(
        _prompt_bias_kernel,
        grid=(MB_HEADS,),
        in_specs=[pl.BlockSpec(memory_space=pltpu.SMEM)],
        out_specs=pl.BlockSpec((1, NEAR_BLOCKS + 1, MB_BLOCK, MB_BLOCK), lambda h: (h, 0, 0, 0)),
        out_shape=jax.ShapeDtypeStruct((MB_HEADS, NEAR_BLOCKS + 1, MB_BLOCK, MB_BLOCK), F32),
        compiler_params=_cparams("parallel"),
        name="prompt_bias",
    )(rel_bias)


def _sample_bias_kernel(rbl_ref, near_ref, new_ref, far_ref, *, t_new):
    n_near = near_ref.shape[0]
    lanes = near_ref.shape[1]
    lookup = lambda bucket: _bucket_lookup(bucket, lambda b: rbl_ref[b:b + 1, :])
    r = lax.broadcasted_iota(jnp.int32, (n_near, lanes), 0)
    tq = lax.broadcasted_iota(jnp.int32, (n_near, lanes), 1) % t_new
    near_ref[...] = lookup(_rel_bucket(jnp.maximum(tq + n_near - r, 0)))
    r = lax.broadcasted_iota(jnp.int32, (t_new, lanes), 0)
    tq = lax.broadcasted_iota(jnp.int32, (t_new, lanes), 1) % t_new
    new_ref[...] = lookup(_rel_bucket(jnp.maximum(tq - r, 0)))
    far = jnp.full((8, lanes), n_near + 1, jnp.int32)
    far_ref[...] = lookup(_rel_bucket(far))


def _sample_bias(rb_lanes, t_new):
    n_near = (NEAR_BLOCKS - 1) * MB_BLOCK
    lanes = rb_lanes.shape[1]
    return pl.pallas_call(
        functools.partial(_sample_bias_kernel, t_new=t_new),
        out_shape=[jax.ShapeDtypeStruct((n_near, lanes), F32), jax.ShapeDtypeStruct((t_new, lanes), F32),
                   jax.ShapeDtypeStruct((8, lanes), F32)],
        name="sample_bias",
    )(rb_lanes)


MOBA_GROUP = 4


def _moba_groups(n_past):
    n_groups = (n_past + MOBA_GROUP - 1) // MOBA_GROUP
    return (n_groups + 1) // 2


def _moba_prompt_kernel(q_ref, k_ref, vt_ref, bias_ref, o_ref, means_s, t_s, raw_s, p_s, *, n_blocks):
    i = pl.program_id(1)
    kblk = lambda j: k_ref[pl.ds(pl.multiple_of(j * MB_BLOCK, MB_BLOCK), MB_BLOCK), :]

    @pl.when(i == 0)
    def _():
        def body(j, carry):
            means_s[pl.ds(j, 1), :] = jnp.sum(kblk(j).astype(F32), axis=0, keepdims=True) * (1.0 / MB_BLOCK)
            return carry
        lax.fori_loop(0, n_blocks, body, 0)

    q = q_ref[...]
    c = MB_DH ** -0.5 * LOG2E
    m_hi, m_lo = _split_hi_lo(means_s[...])
    gate = _dot_nt(m_hi, q) + _dot_nt(m_lo, q)
    blk = lax.broadcasted_iota(jnp.int32, gate.shape, 0)
    gate = jnp.where(blk < i, gate, NEG)
    sel = []
    for _ in range(MB_TOPK):
        mx = jnp.max(gate, axis=0, keepdims=True)
        idx = jnp.min(jnp.where(gate == mx, blk, n_blocks), axis=0, keepdims=True)
        sel.append(jnp.where(idx < i, idx, -1))
        gate = jnp.where(blk == idx, -jnp.inf, gate)

    kj = lax.broadcasted_iota(jnp.int32, (MB_BLOCK, MB_BLOCK), 0)
    qi = lax.broadcasted_iota(jnp.int32, (MB_BLOCK, MB_BLOCK), 1)
    t_own = jnp.where(kj <= qi, _dot_nt(kblk(i), q) * c + bias_ref[0, 0], NEG)
    m = jnp.max(t_own, axis=0, keepdims=True)

    n_trips = _moba_groups(i)
    tiles = lambda g: [(u, g * MOBA_GROUP + u) for u in range(MOBA_GROUP)]
    clamp = lambda jt: jnp.minimum(jt, n_blocks - 1)

    for u, jt in tiles(2 * n_trips):
        t_s[jt] = jnp.full((MB_BLOCK, MB_BLOCK), NEG, F32)

    def scores(g, slot):
        for u, jt in tiles(g):
            raw_s[slot, u] = _dot_nt(kblk(clamp(jt)), q)

    def mask_and_max(g, slot, m):
        for u, jt in tiles(g):
            chosen = (sel[0] == jt) | (sel[1] == jt) | (sel[2] == jt)
            bias = bias_ref[0, jnp.clip(i - jt, 0, NEAR_BLOCKS)]
            t = jnp.where(chosen, raw_s[slot, u] * c + bias, NEG)
            t_s[jt] = t
            m = jnp.maximum(m, jnp.max(t, axis=0, keepdims=True))
        return m

    scores(0, 0)

    def pass1(gg, m):
        m = mask_and_max(2 * gg, 0, m)
        scores(2 * gg + 1, 1)
        m = mask_and_max(2 * gg + 1, 1, m)
        scores(2 * gg + 2, 0)
        return m

    m = lax.fori_loop(0, n_trips, pass1, m)

    def weights(g, slot, l):
        for u, jt in tiles(g):
            p = jnp.exp2(t_s[jt] - m)
            p_s[slot, u] = p.astype(BF16)
            l = l + jnp.sum(p, axis=0, keepdims=True)
        return l

    def accumulate(g, slot, acc):
        for u, jt in tiles(g):
            acc = acc + _dot(vt_ref[clamp(jt)], p_s[slot, u])
        return acc

    p = jnp.exp2(t_own - m)
    l = jnp.sum(p, axis=0, keepdims=True)
    acc = _dot(vt_ref[i], p.astype(BF16))
    l = weights(0, 0, l)

    def pass2(gg, carry):
        l, acc = carry
        acc = accumulate(2 * gg, 0, acc)
        l = weights(2 * gg + 1, 1, l)
        acc = accumulate(2 * gg + 1, 1, acc)
        l = weights(2 * gg + 2, 0, l)
        return l, acc

    l, acc = lax.fori_loop(0, n_trips, pass2, (l, acc))
    o_ref[...] = (acc / l).T.astype(BF16)


def _moba_prompt(aq, ak, av_t, bias_tiles):
    n = aq.shape[0]
    assert n % MB_BLOCK == 0
    nb = n // MB_BLOCK
    assert av_t.shape == (nb, MB_HEADS * MB_DH, MB_BLOCK)
    n_tiles = (2 * _moba_groups(nb - 1) + 1) * MOBA_GROUP
    tile = (MB_BLOCK, MB_BLOCK)
    return pl.pallas_call(
        functools.partial(_moba_prompt_kernel, n_blocks=nb),
        grid=(MB_HEADS, nb),
        in_specs=[pl.BlockSpec((MB_BLOCK, MB_DH), lambda h, i: (i, h)),
                  pl.BlockSpec((n, MB_DH), lambda h, i: (0, h)),
                  pl.BlockSpec((nb, MB_DH, MB_BLOCK), lambda h, i: (0, h, 0)),
                  pl.BlockSpec((1, NEAR_BLOCKS + 1, MB_BLOCK, MB_BLOCK), lambda h, i: (h, 0, 0, 0))],
        out_specs=pl.BlockSpec((MB_BLOCK, MB_DH), lambda h, i: (i, h)),
        out_shape=jax.ShapeDtypeStruct((n, MB_HEADS * MB_DH), BF16),
        scratch_shapes=[pltpu.VMEM((nb, MB_DH), F32), pltpu.VMEM((n_tiles,) + tile, F32),
                        pltpu.VMEM((2, MOBA_GROUP) + tile, F32), pltpu.VMEM((2, MOBA_GROUP) + tile, BF16)],
        compiler_params=_cparams("parallel", "arbitrary"),
        name="moba_prompt",
    )(aq, ak, av_t, bias_tiles)


PAGES_PER_STEP = 8


def _page_heads_on_lanes(ref, page):
    heads = [ref[0, pl.ds(h, page, stride=MB_HEADS), :] for h in range(MB_HEADS)]
    return jnp.concatenate(heads, axis=1).astype(BF16)


def _moba_sample_scores_kernel(pt_ref, qbd_ref, *refs, page, past, t_new):
    k_refs = refs[:PAGES_PER_STEP]
    knew_ref, near_ref, newb_ref, far_ref, p_ref, pnew_ref, l_ref, s_s, gate_s, sel_s = refs[PAGES_PER_STEP:]
    del pt_ref
    step = pl.program_id(1)
    n_steps = pl.num_programs(1)
    nbp = past // MB_BLOCK
    n_far = nbp - (NEAR_BLOCKS - 1)
    lanes = MB_HEADS * t_new
    scale = MB_DH ** -0.5
    qbd = qbd_ref[0]

    for r in range(PAGES_PER_STEP):
        off = pl.multiple_of((step * PAGES_PER_STEP + r) * page, page)
        s_s[pl.ds(off, page), :] = _dot(_page_heads_on_lanes(k_refs[r], page), qbd)

    @pl.when(step == n_steps - 1)
    def _():
        rows = lambda j: pl.ds(pl.multiple_of(j * MB_BLOCK, MB_BLOCK), MB_BLOCK)

        def gate_body(j, carry):
            gate_s[pl.ds(j, 1), :] = jnp.sum(s_s[rows(j), :], axis=0, keepdims=True) * (1.0 / MB_BLOCK)
            return carry
        lax.fori_loop(0, nbp, gate_body, 0)

        gate = gate_s[...]
        bidx = lax.broadcasted_iota(jnp.int32, gate.shape, 0)
        chosen = jnp.zeros(gate.shape, F32)
        for _ in range(MB_TOPK):
            mx = jnp.max(gate, axis=0, keepdims=True)
            idx = jnp.min(jnp.where(gate == mx, bidx, nbp), axis=0, keepdims=True)
            hit = (bidx == idx) & (mx > -jnp.inf)
            chosen = jnp.where(hit, 1.0, chosen)
            gate = jnp.where(bidx == idx, -jnp.inf, gate)
        sel_s[...] = chosen

        far_bias = far_ref[0:1, :]

        def logits(j, bias):
            return jnp.where(sel_s[pl.ds(j, 1), :] > 0.0, s_s[rows(j), :] * scale + bias, NEG)

        def near_bias(j):
            return near_ref[(j - n_far) * MB_BLOCK:(j - n_far + 1) * MB_BLOCK, :]

        tk = lax.broadcasted_iota(jnp.int32, (t_new, lanes), 0)
        tq = lax.broadcasted_iota(jnp.int32, (t_new, lanes), 1) % t_new
        s_new = jnp.where(tk <= tq, _dot(knew_ref[0], qbd) * scale + newb_ref[...], NEG)

        m = jnp.max(s_new, axis=0, keepdims=True)
        m = lax.fori_loop(0, n_far, lambda j, m: jnp.maximum(
            m, jnp.max(logits(j, far_bias), axis=0, keepdims=True)), m)
        for j in range(n_far, nbp):
            m = jnp.maximum(m, jnp.max(logits(j, near_bias(j)), axis=0, keepdims=True))

        e_new = jnp.exp(s_new - m)
        pnew_ref[0] = e_new
        l = jnp.sum(e_new, axis=0, keepdims=True)

        def exp_block(j, bias, l):
            e = jnp.exp(logits(j, bias) - m)
            p_ref[0, rows(j), :] = e.astype(BF16)
            return l + jnp.sum(e, axis=0, keepdims=True)

        l = lax.fori_loop(0, n_far, lambda j, l: exp_block(j, far_bias, l), l)
        for j in range(n_far, nbp):
            l = exp_block(j, near_bias(j), l)
        l_ref[0] = l


def _moba_sample_values_kernel(pt_ref, p_ref, *refs, page, t_new):
    v_refs = refs[:PAGES_PER_STEP]
    pnew_ref, vnew_ref, l_ref, o_ref, acc_s = refs[PAGES_PER_STEP:]
    del pt_ref
    step = pl.program_id(1)

    @pl.when(step == 0)
    def _():
        acc_s[...] = jnp.zeros_like(acc_s)

    for r in range(PAGES_PER_STEP):
        acc_s[...] += _dot_tn(p_ref[0, r * page:(r + 1) * page, :], _page_heads_on_lanes(v_refs[r], page))

    @pl.when(step == pl.num_programs(1) - 1)
    def _():
        acc = acc_s[...] + _dot_tn(pnew_ref[0].astype(BF16), vnew_ref[0])
        acc = acc / l_ref[0]
        parts = [acc[h * t_new:(h + 1) * t_new, h * MB_DH:(h + 1) * MB_DH] for h in range(MB_HEADS)]
        o_ref[0] = jnp.concatenate(parts, axis=1).astype(BF16)


def _moba_sample(aq, ak16, av16, cache_k, cache_v, page_table, near_b, new_b, far_b, batch):
    t_new = aq.shape[0] // batch
    n_pool, page = cache_k.shape[:2]
    n_pages = page_table.shape[1]
    past = n_pages * page
    lanes = MB_HEADS * t_new
    assert n_pages % PAGES_PER_STEP == 0 and past % MB_BLOCK == 0 and past // MB_BLOCK >= NEAR_BLOCKS - 1
    n_steps = n_pages // PAGES_PER_STEP
    ck = cache_k.reshape(n_pool, page * MB_HEADS, MB_DH)
    cv = cache_v.reshape(n_pool, page * MB_HEADS, MB_DH)
    q4 = aq.reshape(batch, t_new, MB_HEADS, MB_DH)
    qbd = jnp.einsum('bthd,hg->bhdgt', q4, jnp.eye(MB_HEADS, dtype=aq.dtype)).reshape(
        batch, MB_HEADS * MB_DH, lanes)

    def page_spec(r):
        return pl.BlockSpec((1, page * MB_HEADS, MB_DH),
                            lambda b, s, pt: (pt[b, s * PAGES_PER_STEP + r], 0, 0))

    per_batch = lambda shape: pl.BlockSpec((1,) + shape, lambda b, s, pt: (b,) + (0,) * len(shape))
    const = lambda shape: pl.BlockSpec(shape, lambda b, s, pt: (0,) * len(shape))

    p, p_new, l = pl.pallas_call(
        functools.partial(_moba_sample_scores_kernel, page=page, past=past, t_new=t_new),
        grid_spec=pltpu.PrefetchScalarGridSpec(
            num_scalar_prefetch=1,
            grid=(batch, n_steps),
            in_specs=[per_batch((MB_HEADS * MB_DH, lanes))] + [page_spec(r) for r in range(PAGES_PER_STEP)]
                     + [per_batch((t_new, MB_HEADS * MB_DH)), const(near_b.shape), const(new_b.shape),
                        const(far_b.shape)],
            out_specs=[per_batch((past, lanes)), per_batch((t_new, lanes)), per_batch((1, lanes))],
            scratch_shapes=[pltpu.VMEM((past, lanes), F32), pltpu.VMEM((past // MB_BLOCK, lanes), F32),
                            pltpu.VMEM((past // MB_BLOCK, lanes), F32)]),
        out_shape=[jax.ShapeDtypeStruct((batch, past, lanes), BF16),
                   jax.ShapeDtypeStruct((batch, t_new, lanes), F32),
                   jax.ShapeDtypeStruct((batch, 1, lanes), F32)],
        compiler_params=_cparams("parallel", "arbitrary"),
        name="moba_sample_scores",
    )(page_table, qbd, *([ck] * PAGES_PER_STEP), ak16.reshape(batch, t_new, -1), near_b, new_b, far_b)

    y = pl.pallas_call(
        functools.partial(_moba_sample_values_kernel, page=page, t_new=t_new),
        grid_spec=pltpu.PrefetchScalarGridSpec(
            num_scalar_prefetch=1,
            grid=(batch, n_steps),
            in_specs=[pl.BlockSpec((1, PAGES_PER_STEP * page, lanes), lambda b, s, pt: (b, s, 0))]
                     + [page_spec(r) for r in range(PAGES_PER_STEP)]
                     + [per_batch((t_new, lanes)), per_batch((t_new, MB_HEADS * MB_DH)), per_batch((lanes, 1))],
            out_specs=per_batch((t_new, MB_HEADS * MB_DH)),
            scratch_shapes=[pltpu.VMEM((lanes, MB_HEADS * MB_DH), F32)]),
        out_shape=jax.ShapeDtypeStruct((batch, t_new, MB_HEADS * MB_DH), BF16),
        compiler_params=_cparams("parallel", "arbitrary"),
        name="moba_sample_values",
    )(page_table, p, *([cv] * PAGES_PER_STEP), p_new, av16.reshape(batch, t_new, -1),
      l.reshape(batch, lanes, 1))
    return y.reshape(batch * t_new, MB_HEADS * MB_DH)


def _memkv_kernel(mem_ref, g_ref, w_ref, kn_ref, mk_ref, mv_ref):
    xb = _rms(mem_ref[...], g_ref[...]).astype(BF16)
    width = X_HEADS * X_DH
    mk_ref[...] = _head_rms(_dot(xb, w_ref[:, :width]), kn_ref[...], X_DH)
    mv_ref[...] = _dot(xb, w_ref[:, width:])


def _memkv(mem, mem_norm_g, w_mem_kv, cross_kn_g):
    m = mem.shape[0]
    width = X_HEADS * X_DH
    return pl.pallas_call(
        _memkv_kernel,
        out_shape=[jax.ShapeDtypeStruct((m, width), F32), jax.ShapeDtypeStruct((m, width), F32)],
        compiler_params=pltpu.CompilerParams(vmem_limit_bytes=VMEM_LIMIT),
        name="memkv",
    )(mem, mem_norm_g, w_mem_kv, cross_kn_g)


def _cross_kernel(q_ref, mk_ref, mv_ref, o_ref):
    scale = X_DH ** -0.5
    for h in range(X_HEADS):
        cols = slice(h * X_DH, (h + 1) * X_DH)
        s = _dot_nt(q_ref[0, :, cols], mk_ref[0, :, cols].astype(BF16)) * scale
        p = jnp.exp(s - jnp.max(s, axis=1, keepdims=True))
        p = p / jnp.sum(p, axis=1, keepdims=True)
        o_ref[0, :, cols] = _dot(p.astype(BF16), mv_ref[0, :, cols].astype(BF16)).astype(BF16)


def _cross(cq, mk, mv, batch, tq):
    n = cq.shape[0]
    t = n // batch
    assert t % tq == 0
    nt = t // tq
    width = X_HEADS * X_DH
    n_mem = mk.shape[1]
    q_spec = pl.BlockSpec((1, tq, width), lambda b, i: (b * nt + i, 0, 0))
    mem_spec = pl.BlockSpec((1, n_mem, width), lambda b, i: (b, 0, 0))
    y = pl.pallas_call(
        _cross_kernel,
        grid=(batch, nt),
        in_specs=[q_spec, mem_spec, mem_spec],
        out_specs=q_spec,
        out_shape=jax.ShapeDtypeStruct((batch * nt, tq, width), BF16),
        compiler_params=_cparams("parallel", "arbitrary"),
        name="cross",
    )(cq.reshape(batch * nt, tq, width), mk, mv)
    return y.reshape(n, width)


def _merge_kernel(x_ref, ym_ref, ya_ref, yc_ref, g_ref, wm_ref, wa_ref, wc_ref, wo_ref, o_ref):
    mixed = (g_ref[:, 0:1024] * _dot(ym_ref[...], wm_ref[...])
             + g_ref[:, 1024:2048] * _dot(ya_ref[...], wa_ref[...])
             + g_ref[:, 2048:3072] * _dot(yc_ref[...], wc_ref[...]))
    o_ref[...] = x_ref[...] + _dot(mixed.astype(BF16), wo_ref[...])


def _merge(x, y_m, y_a, y_c, g, w_m, w_a, w_c, w_o, tm=256):
    n = x.shape[0]
    tm = min(tm, n)
    assert n % tm == 0
    row = lambda w: pl.BlockSpec((tm, w), lambda i: (i, 0))
    return pl.pallas_call(
        _merge_kernel,
        grid=(n // tm,),
        in_specs=[row(1024), row(1024), row(1024), row(1024), row(3072)]
                 + [_resident((1024, 1024))] * 4,
        out_specs=row(1024),
        out_shape=jax.ShapeDtypeStruct((n, 1024), F32),
        compiler_params=_cparams("parallel"),
        name="merge",
    )(x, y_m, y_a, y_c, g, w_m, w_a, w_c, w_o)


def _swiglu_kernel(x_ref, g_ref, wgu_ref, wd_ref, o_ref):
    x = x_ref[...]
    gu = _dot(_rms(x, g_ref[...]).astype(BF16), wgu_ref[...])
    d_ff = wd_ref.shape[0]
    gt, up = gu[:, :d_ff], gu[:, d_ff:]
    o_ref[...] = x + _dot((gt * jax.nn.sigmoid(gt) * up).astype(BF16), wd_ref[...])


def _swiglu(x, norm2_g, w_gu, w_down, tm=256):
    n = x.shape[0]
    tm = min(tm, n)
    assert n % tm == 0
    row = pl.BlockSpec((tm, 1024), lambda i: (i, 0))
    return pl.pallas_call(
        _swiglu_kernel,
        grid=(n // tm,),
        in_specs=[row, _resident((1, 1024)), _resident(w_gu.shape), _resident(w_down.shape)],
        out_specs=row,
        out_shape=jax.ShapeDtypeStruct((n, 1024), F32),
        compiler_params=_cparams("parallel"),
        name="swiglu",
    )(x, norm2_g, w_gu, w_down)


def _pick_chunk(t):
    for c in (256, 128, 64):
        if t % c == 0:
            return c
    return t


def _layer(x3, ml_state, mem_k, mem_v, moba, w):
    batch, t, _ = x3.shape
    x = x3.reshape(batch * t, D_MODEL)
    (qk, v, o_sig, aq, ak32, ak16, av32, av16, av_t, cq, g, gif) = _inproj(
        x, w['norm1_g'], w['w_main'], w['w_if'], w['b_if'], w['moba_qn_g'], w['moba_kn_g'], w['cross_qn_g'])
    y_m, c_out, n_out, m_out = _mlstm(qk, v, o_sig, gif, ml_state[0], ml_state[1], ml_state[2],
                                      w['mlstm_hn_g'], batch, _pick_chunk(t))
    y_a = moba(aq, ak16, av16, av_t)
    y_c = _cross(cq, mem_k, mem_v, batch, min(t, 256))
    x1 = _merge(x, y_m, y_a, y_c, g, w['w_br_m'], w['w_br_a'], w['w_br_c'], w['w_out'])
    y = _swiglu(x1, w['norm2_g'], w['w_gu'], w['w_down'])
    return (y.reshape(batch, t, D_MODEL), ak32.reshape(batch, t, MB_HEADS, MB_DH),
            av32.reshape(batch, t, MB_HEADS, MB_DH), c_out, n_out, m_out.reshape(batch, ML_HEADS))


def kernel(x_prompt, x_sample, mem_prompt, cache_k, cache_v, cache_mem_k, cache_mem_v, state_C, state_n, state_m, page_table, norm1_g, w_in, b_ig, b_fg, mlstm_hn_g, moba_qn_g, moba_kn_g, rel_bias, cross_qn_g, cross_kn_g, mem_norm_g, w_mem_kv, w_br_m, w_br_a, w_br_c, w_out, norm2_g, w_gu, w_down):
    c_if = 2 * ML_HEADS * ML_DQK + 2 * ML_HEADS * ML_DV
    row = lambda a: a.reshape(1, -1)
    w = {
        'norm1_g': row(norm1_g),
        'w_main': jnp.concatenate([w_in[:, :c_if], w_in[:, c_if + 2 * ML_HEADS:]], axis=1).astype(BF16),
        'w_if': w_in[:, c_if:c_if + 2 * ML_HEADS],
        'b_if': row(jnp.concatenate([b_ig, b_fg])),
        'mlstm_hn_g': row(mlstm_hn_g), 'moba_qn_g': row(moba_qn_g), 'moba_kn_g': row(moba_kn_g),
        'cross_qn_g': row(cross_qn_g),
        'w_br_m': w_br_m.astype(BF16), 'w_br_a': w_br_a.astype(BF16), 'w_br_c': w_br_c.astype(BF16),
        'w_out': w_out.astype(BF16), 'norm2_g': row(norm2_g),
        'w_gu': w_gu.astype(BF16), 'w_down': w_down.astype(BF16),
    }
    assert w['w_main'].shape[1] == N_SEG * 1024

    bp, sp, _ = x_prompt.shape
    assert bp == 1
    zero_state = (jnp.zeros((bp, ML_HEADS, ML_DQK, ML_DV), F32), jnp.zeros((bp, ML_HEADS, ML_DQK), F32),
                  jnp.zeros((bp, ML_HEADS), F32))
    n_mem = mem_prompt.shape[1]
    mem_k_p, mem_v_p = _memkv(mem_prompt.reshape(bp * n_mem, D_MODEL), row(mem_norm_g),
                              w_mem_kv.astype(BF16), row(cross_kn_g))
    mem_k_p = mem_k_p.reshape(bp, n_mem, X_HEADS * X_DH)
    mem_v_p = mem_v_p.reshape(bp, n_mem, X_HEADS * X_DH)
    bias_tiles = _prompt_bias(rel_bias)
    y_p, k_p, v_p, c_p, n_p, m_p = _layer(
        x_prompt, zero_state, mem_k_p, mem_v_p,
        lambda aq, ak, av, av_t: _moba_prompt(aq, ak, av_t, bias_tiles), w)

    bs, ts, _ = x_sample.shape
    rb_lanes = jnp.repeat(rel_bias, ts, axis=1)
    near_b, new_b, far_b = _sample_bias(rb_lanes, ts)
    y_s, k_s, v_s, c_s, n_s, m_s = _layer(
        x_sample, (state_C, state_n, state_m),
        cache_mem_k.reshape(bs, n_mem, X_HEADS * X_DH), cache_mem_v.reshape(bs, n_mem, X_HEADS * X_DH),
        lambda aq, ak, av, av_t: _moba_sample(aq, ak, av, cache_k, cache_v, page_table, near_b, new_b, far_b, bs),
        w)

    return (y_p, y_s, k_p, v_p, mem_k_p.reshape(bp, n_mem, X_HEADS, X_DH),
            mem_v_p.reshape(bp, n_mem, X_HEADS, X_DH), c_p, n_p, m_p, k_s, v_s, c_s, n_s, m_s)
```

```python
import functools
import math

import jax
import jax.numpy as jnp
from jax import lax
from jax.experimental import pallas as pl
from jax.experimental.pallas import tpu as pltpu

F32 = jnp.float32
BF16 = jnp.bfloat16

EPS = 1e-6
D_MODEL = 1024
ML_HEADS, ML_DQK, ML_DV = 4, 128, 256
GATE_CAP = 15.0
MB_HEADS, MB_DH, MB_BLOCK, MB_TOPK = 8, 128, 256, 3
X_HEADS, X_DH = 4, 256
REL_BUCKETS, REL_MAX_DIST = 32, 1024
NEG = -1e30
LOG2E = math.log2(math.e)
NEAR_BLOCKS = 5
N_SEG = 10

VMEM_LIMIT = 56 * 1024 * 1024


def _cparams(*sem):
    return pltpu.CompilerParams(dimension_semantics=sem, vmem_limit_bytes=VMEM_LIMIT)


def _resident(shape):
    nd = len(shape)
    return pl.BlockSpec(shape, lambda *_: (0,) * nd, pipeline_mode=pl.Buffered(1))


def _rms(x, g):
    return x * lax.rsqrt(jnp.mean(x * x, axis=-1, keepdims=True) + EPS) * g


def _head_rms(x, g, width):
    parts = [_rms(x[:, s:s + width], g) for s in range(0, x.shape[1], width)]
    return jnp.concatenate(parts, axis=1)


def _dot(a, b):
    return jnp.dot(a, b, preferred_element_type=F32)


def _dot_nt(a, b):
    return lax.dot_general(a, b, (((1,), (1,)), ((), ())), preferred_element_type=F32)


def _dot_tn(a, b):
    return lax.dot_general(a, b, (((0,), (0,)), ((), ())), preferred_element_type=F32)


def _split_hi_lo(x):
    hi = x.astype(BF16)
    lo = (x - hi.astype(F32)).astype(BF16)
    return hi, lo


def _rel_bucket(dist):
    max_exact = REL_BUCKETS // 2
    d = jnp.maximum(dist, 1).astype(F32)
    large = max_exact + (jnp.log(d / max_exact) / math.log(REL_MAX_DIST / max_exact)
                         * (REL_BUCKETS - max_exact)).astype(jnp.int32)
    large = jnp.minimum(large, REL_BUCKETS - 1)
    return jnp.where(dist < max_exact, dist, large)


def _inproj_kernel(x_ref, g1_ref, w_ref, wif_ref, bif_ref, qn_ref, kn_ref, cqn_ref,
                   qk_ref, v_ref, o_ref, aq_ref, ak32_ref, ak16_ref, av32_ref, av16_ref, avt_ref,
                   cq_ref, g_ref, gif_ref):
    xn = _rms(x_ref[...], g1_ref[...])
    xb = xn.astype(BF16)

    def seg(s):
        return _dot(xb, w_ref[:, s * 1024:(s + 1) * 1024])

    lane = lax.broadcasted_iota(jnp.int32, (1, 1024), 1)
    kscale = jnp.where(lane >= ML_HEADS * ML_DQK, ML_DQK ** -0.5, 1.0).astype(F32)
    qk_ref[...] = (seg(0) * kscale).astype(BF16)
    v_ref[...] = seg(1).astype(BF16)
    o_ref[...] = jax.nn.sigmoid(seg(2))
    aq_ref[...] = _head_rms(seg(3), qn_ref[...], MB_DH).astype(BF16)
    ak = _head_rms(seg(4), kn_ref[...], MB_DH)
    ak32_ref[...] = ak
    ak16_ref[...] = ak.astype(BF16)
    av = seg(5)
    av32_ref[...] = av
    av16_ref[...] = av.astype(BF16)
    avt_ref[0] = av.T.astype(BF16)
    cq_ref[...] = _head_rms(seg(6), cqn_ref[...], X_DH).astype(BF16)
    for s in range(3):
        g_ref[:, s * 1024:(s + 1) * 1024] = jax.nn.sigmoid(seg(7 + s))

    x_lo = (xn - xb.astype(F32)).astype(BF16)
    w_hi, w_lo = _split_hi_lo(wif_ref[...])
    z = _dot(xb, w_hi) + (_dot(x_lo, w_hi) + _dot(xb, w_lo)) + bif_ref[...]
    capped = GATE_CAP * jnp.tanh(z / GATE_CAP)
    log_sig = -(jnp.maximum(-capped, 0.0) + jnp.log1p(jnp.exp(-jnp.abs(capped))))
    col = lax.broadcasted_iota(jnp.int32, z.shape, 1)
    gif_ref[...] = jnp.where(col < ML_HEADS, capped, log_sig)


def _inproj(x, norm1_g, w_main, w_if, b_if, qn_g, kn_g, cqn_g, tm=256):
    n = x.shape[0]
    tm = min(tm, n)
    assert n % tm == 0
    row = lambda w: pl.BlockSpec((tm, w), lambda i: (i, 0))
    shapes = [(1024, BF16), (1024, BF16), (1024, F32), (1024, BF16), (1024, F32), (1024, BF16),
              (1024, F32), (1024, BF16), None, (1024, BF16), (3072, F32), (2 * ML_HEADS, F32)]
    out_specs = [pl.BlockSpec((1, 1024, tm), lambda i: (i, 0, 0)) if s is None else row(s[0]) for s in shapes]
    out_shape = [jax.ShapeDtypeStruct((n // tm, 1024, tm), BF16) if s is None
                 else jax.ShapeDtypeStruct((n, s[0]), s[1]) for s in shapes]
    return pl.pallas_call(
        _inproj_kernel,
        grid=(n // tm,),
        in_specs=[row(1024), _resident((1, 1024)), _resident(w_main.shape), _resident(w_if.shape),
                  _resident((1, 2 * ML_HEADS)), _resident((1, MB_DH)), _resident((1, MB_DH)),
                  _resident((1, X_DH))],
        out_specs=out_specs,
        out_shape=out_shape,
        compiler_params=_cparams("parallel"),
        name="inproj",
    )(x, norm1_g, w_main, w_if, b_if, qn_g, kn_g, cqn_g)


def _mlstm_kernel(qk_ref, v_ref, o_ref, gc_ref, gr_ref, c0_ref, n0_ref, m0_ref, hn_ref,
                  y_ref, cout_ref, nout_ref, mout_ref, c_s, n_s, m_s, *, chunk, n_chunks):
    L = chunk
    c = pl.program_id(1)

    @pl.when(c == 0)
    def _():
        c_s[...] = c0_ref[0]
        n_s[...] = n0_ref[0]
        m_s[...] = m0_ref[0]

    gc = gc_ref[0]
    gr = gr_ref[0]
    t_idx = lax.broadcasted_iota(jnp.int32, (L, L), 0)
    s_idx = lax.broadcasted_iota(jnp.int32, (L, L), 1)
    causal = s_idx <= t_idx

    for h in range(ML_HEADS):
        ig_row, lf_row = gr[h:h + 1, :], gr[ML_HEADS + h:ML_HEADS + h + 1, :]
        ig_col, lf_col = gc[:, h:h + 1], gc[:, ML_HEADS + h:ML_HEADS + h + 1]
        b_col = jnp.sum(jnp.where(causal, lf_row, 0.0), axis=1, keepdims=True)
        b_row = jnp.sum(jnp.where(t_idx <= s_idx, lf_col, 0.0), axis=0, keepdims=True)
        m_prev = m_s[:, h:h + 1]

        log_d = jnp.where(causal, b_col - b_row + ig_row, -jnp.inf)
        m_inter = b_col + m_prev
        m_t = jnp.maximum(m_inter, jnp.max(log_d, axis=1, keepdims=True))
        w_intra = jnp.exp(log_d - m_t)
        w_inter = jnp.exp(m_inter - m_t)

        q = qk_ref[0, :, h * ML_DQK:(h + 1) * ML_DQK]
        k = qk_ref[0, :, (ML_HEADS + h) * ML_DQK:(ML_HEADS + h + 1) * ML_DQK]
        v = v_ref[0, :, h * ML_DV:(h + 1) * ML_DV]
        c_h = c_s[h]
        n_h = n_s[h:h + 1, :]

        s = _dot_nt(q, k) * w_intra
        num = w_inter * _dot(q, c_h.astype(BF16)) + _dot(s.astype(BF16), v)
        qn = jnp.sum(q.astype(F32) * n_h, axis=1, keepdims=True)
        den = w_inter * qn + jnp.sum(s, axis=1, keepdims=True)
        hh = num / jnp.maximum(jnp.abs(den), jnp.exp(-m_t))

        b_last = b_col[L - 1:L, :]
        m_new = jnp.maximum(b_last + m_prev,
                            jnp.max(b_last - b_row + ig_row, axis=1, keepdims=True))
        w_col = jnp.exp(b_last - b_col + ig_col - m_new)
        decay = jnp.exp(b_last + m_prev - m_new)
        kw = k.astype(F32) * w_col
        c_s[h] = decay * c_h + _dot_tn(kw.astype(BF16), v)
        n_s[h:h + 1, :] = decay * n_h + jnp.sum(kw, axis=0, keepdims=True)
        m_s[:, h:h + 1] = m_new

        hn = _rms(hh, hn_ref[:, h * ML_DV:(h + 1) * ML_DV])
        y_ref[0, :, h * ML_DV:(h + 1) * ML_DV] = (o_ref[0, :, h * ML_DV:(h + 1) * ML_DV] * hn).astype(BF16)

    @pl.when(c == n_chunks - 1)
    def _():
        cout_ref[0] = c_s[...]
        nout_ref[0] = n_s[...]
        mout_ref[0] = m_s[...]


def _mlstm(qk, v, o_sig, gif, c0, n0, m0, hn_g, batch, chunk):
    n = qk.shape[0]
    t = n // batch
    nc = t // chunk
    chunked = lambda a: a.reshape(batch * nc, chunk, a.shape[-1])
    gc = chunked(gif)
    gr = gc.transpose(0, 2, 1)
    row = lambda w: pl.BlockSpec((1, chunk, w), lambda b, c: (b * nc + c, 0, 0))
    state = lambda shape: pl.BlockSpec((1,) + shape, lambda b, c: (b,) + (0,) * len(shape))
    y, c_out, n_out, m_out = pl.pallas_call(
        functools.partial(_mlstm_kernel, chunk=chunk, n_chunks=nc),
        grid=(batch, nc),
        in_specs=[row(1024), row(1024), row(1024), row(2 * ML_HEADS),
                  pl.BlockSpec((1, 2 * ML_HEADS, chunk), lambda b, c: (b * nc + c, 0, 0)),
                  state((ML_HEADS, ML_DQK, ML_DV)), state((ML_HEADS, ML_DQK)), state((1, ML_HEADS)),
                  _resident((1, ML_HEADS * ML_DV))],
        out_specs=[row(1024), state((ML_HEADS, ML_DQK, ML_DV)), state((ML_HEADS, ML_DQK)),
                   state((1, ML_HEADS))],
        out_shape=[jax.ShapeDtypeStruct((batch * nc, chunk, 1024), BF16),
                   jax.ShapeDtypeStruct((batch, ML_HEADS, ML_DQK, ML_DV), F32),
                   jax.ShapeDtypeStruct((batch, ML_HEADS, ML_DQK), F32),
                   jax.ShapeDtypeStruct((batch, 1, ML_HEADS), F32)],
        scratch_shapes=[pltpu.VMEM((ML_HEADS, ML_DQK, ML_DV), F32), pltpu.VMEM((ML_HEADS, ML_DQK), F32),
                        pltpu.VMEM((1, ML_HEADS), F32)],
        compiler_params=_cparams("parallel", "arbitrary"),
        name="mlstm",
    )(chunked(qk), chunked(v), chunked(o_sig), gc, gr, c0, n0, m0.reshape(batch, 1, ML_HEADS), hn_g)
    return y.reshape(n, 1024), c_out, n_out, m_out


def _bucket_lookup(bucket, table_row):
    val = jnp.zeros(bucket.shape, F32)
    for b in range(REL_BUCKETS):
        val = jnp.where(bucket == b, table_row(b), val)
    return val


def _prompt_bias_kernel(rb_ref, o_ref):
    h = pl.program_id(0)
    kj = lax.broadcasted_iota(jnp.int32, (MB_BLOCK, MB_BLOCK), 0)
    qi = lax.broadcasted_iota(jnp.int32, (MB_BLOCK, MB_BLOCK), 1)
    for d in range(NEAR_BLOCKS + 1):
        bucket = _rel_bucket(jnp.maximum(d * MB_BLOCK + qi - kj, 0))
        bias = _bucket_lookup(bucket, lambda b: rb_ref[b, h]) * LOG2E
        o_ref[0, d] = jnp.where(kj <= qi, bias, NEG) if d == 0 else bias


def _prompt_bias(rel_bias):
    return pl.pallas_call(
        _prompt_bias_kernel,
        grid=(MB_HEADS,),
        in_specs=[pl.BlockSpec(memory_space=pltpu.SMEM)],
        out_specs=pl.BlockSpec((1, NEAR_BLOCKS + 1, MB_BLOCK, MB_BLOCK), lambda h: (h, 0, 0, 0)),
        out_shape=jax.ShapeDtypeStruct((MB_HEADS, NEAR_BLOCKS + 1, MB_BLOCK, MB_BLOCK), F32),
        compiler_params=_cparams("parallel"),
        name="prompt_bias",
    )(rel_bias)


N_NEAR = (NEAR_BLOCKS - 1) * MB_BLOCK


def _sample_bias_kernel(rbl_ref, cached_ref, new_ref, *, page, t_new):
    lanes = cached_ref.shape[1]
    lookup = lambda bucket: _bucket_lookup(bucket, lambda b: rbl_ref[b:b + 1, :])
    r = lax.broadcasted_iota(jnp.int32, cached_ref.shape, 0)
    tq = lax.broadcasted_iota(jnp.int32, cached_ref.shape, 1) % t_new
    cached_ref[...] = lookup(_rel_bucket(jnp.where(r < page, N_NEAR + 1, tq + N_NEAR + page - r)))
    r = lax.broadcasted_iota(jnp.int32, (t_new, lanes), 0)
    tq = lax.broadcasted_iota(jnp.int32, (t_new, lanes), 1) % t_new
    new_ref[...] = lookup(_rel_bucket(jnp.maximum(tq - r, 0)))


def _sample_bias(rb_lanes, page, t_new):
    lanes = rb_lanes.shape[1]
    return pl.pallas_call(
        functools.partial(_sample_bias_kernel, page=page, t_new=t_new),
        out_shape=[jax.ShapeDtypeStruct((page + N_NEAR, lanes), F32), jax.ShapeDtypeStruct((t_new, lanes), F32)],
        name="sample_bias",
    )(rb_lanes)


MOBA_GROUP = 4
MOBA_RUNS = (8, 4, 2, 1)
ONES_ROWS = 16


def _run_lengths(n_groups, fn, carry):
    done = 0
    for idx, run in enumerate(MOBA_RUNS):
        trips = (n_groups - done) // run if idx == 0 else ((n_groups - done) // run) % 2
        first = done

        def body(r, c, first=first, run=run):
            return fn((first + r * run) * MOBA_GROUP, run * MOBA_GROUP, c)
        carry = lax.fori_loop(0, trips, body, carry)
        done = done + trips * run
    return carry


def _moba_prompt_kernel(q_ref, k_ref, vt_ref, bias_ref, o_ref, means_s, t_s, *, n_blocks):
    i = pl.program_id(1)
    kblk = lambda j: k_ref[pl.ds(pl.multiple_of(j * MB_BLOCK, MB_BLOCK), MB_BLOCK), :]

    @pl.when(i == 0)
    def _():
        def body(j, carry):
            means_s[pl.ds(j, 1), :] = jnp.sum(kblk(j).astype(F32), axis=0, keepdims=True) * (1.0 / MB_BLOCK)
            return carry
        lax.fori_loop(0, n_blocks, body, 0)

    q = q_ref[...]
    c = MB_DH ** -0.5 * LOG2E
    m_hi, m_lo = _split_hi_lo(means_s[...])
    gate = _dot_nt(m_hi, q) + _dot_nt(m_lo, q)
    blk = lax.broadcasted_iota(jnp.int32, gate.shape, 0)
    gate = jnp.where(blk < i, gate, NEG)
    sel = []
    for _ in range(MB_TOPK):
        mx = jnp.max(gate, axis=0, keepdims=True)
        idx = jnp.min(jnp.where(gate == mx, blk, n_blocks), axis=0, keepdims=True)
        sel.append(jnp.where(idx < i, idx, -1))
        gate = jnp.where(blk == idx, -jnp.inf, gate)

    n_groups = (i + MOBA_GROUP) // MOBA_GROUP
    clamp = lambda jt: jnp.minimum(jt, n_blocks - 1)

    def logits_and_max(first, count, m):
        for u in range(count):
            jt = first + u
            chosen = (sel[0] == jt) | (sel[1] == jt) | (sel[2] == jt) | (jt == i)
            bias = bias_ref[0, jnp.clip(i - jt, 0, NEAR_BLOCKS)]
            t = jnp.where(chosen, _dot_nt(kblk(clamp(jt)), q) * c + bias, NEG)
            t_s[jt] = t
            m = jnp.maximum(m, jnp.max(t, axis=0, keepdims=True))
        return m

    m = _run_lengths(n_groups, logits_and_max, jnp.full((1, MB_BLOCK), NEG, F32))

    ones = jnp.ones((ONES_ROWS, MB_BLOCK), BF16)

    def accumulate(first, count, acc):
        for u in range(count):
            jt = first + u
            p = jnp.exp2((t_s[jt] - m).astype(BF16))
            acc = acc + _dot(jnp.concatenate([vt_ref[clamp(jt)], ones], axis=0), p)
        return acc

    acc = _run_lengths(n_groups, accumulate, jnp.zeros((MB_DH + ONES_ROWS, MB_BLOCK), F32))
    o_ref[...] = (acc[:MB_DH] / acc[MB_DH:MB_DH + 1]).T.astype(BF16)


def _moba_prompt(aq, ak, av_t, bias_tiles):
    n = aq.shape[0]
    assert n % MB_BLOCK == 0
    nb = n // MB_BLOCK
    assert av_t.shape == (nb, MB_HEADS * MB_DH, MB_BLOCK)
    n_tiles = -(-nb // MOBA_GROUP) * MOBA_GROUP
    return pl.pallas_call(
        functools.partial(_moba_prompt_kernel, n_blocks=nb),
        grid=(MB_HEADS, nb),
        in_specs=[pl.BlockSpec((MB_BLOCK, MB_DH), lambda h, i: (i, h)),
                  pl.BlockSpec((n, MB_DH), lambda h, i: (0, h)),
                  pl.BlockSpec((nb, MB_DH, MB_BLOCK), lambda h, i: (0, h, 0)),
                  pl.BlockSpec((1, NEAR_BLOCKS + 1, MB_BLOCK, MB_BLOCK), lambda h, i: (h, 0, 0, 0))],
        out_specs=pl.BlockSpec((MB_BLOCK, MB_DH), lambda h, i: (i, h)),
        out_shape=jax.ShapeDtypeStruct((n, MB_HEADS * MB_DH), BF16),
        scratch_shapes=[pltpu.VMEM((nb, MB_DH), F32), pltpu.VMEM((n_tiles, MB_BLOCK, MB_BLOCK), F32)],
        compiler_params=_cparams("parallel", "arbitrary"),
        name="moba_prompt",
    )(aq, ak, av_t, bias_tiles)


PAGES_PER_STEP = 8


def _page_heads_on_lanes(ref, page):
    heads = [ref[0, pl.ds(h, page, stride=MB_HEADS), :] for h in range(MB_HEADS)]
    return jnp.concatenate(heads, axis=1).astype(BF16)


def _moba_sample_scores_kernel(pt_ref, qbd_ref, *refs, page, past, t_new):
    k_refs = refs[:PAGES_PER_STEP]
    knew_ref, bias_ref, newb_ref, p_ref, pnew_ref, l_ref, t_s, gate_s, bmax_s = refs[PAGES_PER_STEP:]
    del pt_ref
    step = pl.program_id(1)
    n_steps = pl.num_programs(1)
    nbp = past // MB_BLOCK
    ppb = MB_BLOCK // page
    n_far_pages = (past - N_NEAR) // page
    lanes = MB_HEADS * t_new
    scale = MB_DH ** -0.5
    qbd = qbd_ref[0]

    for b in range(PAGES_PER_STEP // ppb):
        gate, bmax = 0.0, NEG
        for r in range(b * ppb, (b + 1) * ppb):
            pg = step * PAGES_PER_STEP + r
            s = _dot(_page_heads_on_lanes(k_refs[r], page), qbd)
            brow = pl.multiple_of(jnp.maximum(pg - n_far_pages + 1, 0) * page, page)
            t = s * scale + bias_ref[pl.ds(brow, page), :]
            t_s[pl.ds(pl.multiple_of(pg * page, page), page), :] = t
            gate = gate + jnp.sum(s, axis=0, keepdims=True)
            bmax = jnp.maximum(bmax, jnp.max(t, axis=0, keepdims=True))
        blk = step * (PAGES_PER_STEP // ppb) + b
        gate_s[pl.ds(blk, 1), :] = gate * (1.0 / MB_BLOCK)
        bmax_s[pl.ds(blk, 1), :] = bmax

    @pl.when(step == n_steps - 1)
    def _():
        gate = gate_s[...]
        bidx = lax.broadcasted_iota(jnp.int32, gate.shape, 0)
        chosen = jnp.zeros(gate.shape, jnp.bool_)
        for _ in range(MB_TOPK):
            mx = jnp.max(gate, axis=0, keepdims=True)
            idx = jnp.min(jnp.where(gate == mx, bidx, nbp), axis=0, keepdims=True)
            chosen = chosen | ((bidx == idx) & (mx > -jnp.inf))
            gate = jnp.where(bidx == idx, -jnp.inf, gate)

        tk = lax.broadcasted_iota(jnp.int32, (t_new, lanes), 0)
        tq = lax.broadcasted_iota(jnp.int32, (t_new, lanes), 1) % t_new
        t_new_keys = jnp.where(tk <= tq, _dot(knew_ref[0], qbd) * scale + newb_ref[...], NEG)

        m = jnp.maximum(jnp.max(t_new_keys, axis=0, keepdims=True),
                        jnp.max(jnp.where(chosen, bmax_s[...], NEG), axis=0, keepdims=True))
        e_new = jnp.exp(t_new_keys - m)
        pnew_ref[0] = e_new
        gate_s[...] = jnp.where(chosen, 0.0, NEG)

        def exp_block(j, l):
            rows = pl.ds(pl.multiple_of(j * MB_BLOCK, MB_BLOCK), MB_BLOCK)
            e = jnp.exp(jnp.minimum(t_s[rows, :] - m, 0.0) + gate_s[pl.ds(j, 1), :])
            p_ref[0, rows, :] = e.astype(BF16)
            return l + jnp.sum(e, axis=0, keepdims=True)

        l_ref[0] = lax.fori_loop(0, nbp, exp_block, jnp.sum(e_new, axis=0, keepdims=True))


def _moba_sample_values_kernel(pt_ref, p_ref, *refs, page, t_new):
    v_refs = refs[:PAGES_PER_STEP]
    pnew_ref, vnew_ref, l_ref, o_ref, acc_s = refs[PAGES_PER_STEP:]
    del pt_ref
    step = pl.program_id(1)

    @pl.when(step == 0)
    def _():
        acc_s[...] = jnp.zeros_like(acc_s)

    for r in range(PAGES_PER_STEP):
        acc_s[...] += _dot_tn(p_ref[0, r * page:(r + 1) * page, :], _page_heads_on_lanes(v_refs[r], page))

    @pl.when(step == pl.num_programs(1) - 1)
    def _():
        acc = acc_s[...] + _dot_tn(pnew_ref[0].astype(BF16), vnew_ref[0])
        acc = acc / l_ref[0]
        parts = [acc[h * t_new:(h + 1) * t_new, h * MB_DH:(h + 1) * MB_DH] for h in range(MB_HEADS)]
        o_ref[0] = jnp.concatenate(parts, axis=1).astype(BF16)


def _moba_sample(aq, ak16, av16, cache_k, cache_v, page_table, cached_b, new_b, batch):
    t_new = aq.shape[0] // batch
    n_pool, page = cache_k.shape[:2]
    n_pages = page_table.shape[1]
    past = n_pages * page
    lanes = MB_HEADS * t_new
    assert MB_BLOCK % page == 0 and PAGES_PER_STEP % (MB_BLOCK // page) == 0
    assert n_pages % PAGES_PER_STEP == 0 and past % MB_BLOCK == 0 and past >= N_NEAR
    n_steps = n_pages // PAGES_PER_STEP
    ck = cache_k.reshape(n_pool, page * MB_HEADS, MB_DH)
    cv = cache_v.reshape(n_pool, page * MB_HEADS, MB_DH)
    q4 = aq.reshape(batch, t_new, MB_HEADS, MB_DH)
    qbd = jnp.einsum('bthd,hg->bhdgt', q4, jnp.eye(MB_HEADS, dtype=aq.dtype)).reshape(
        batch, MB_HEADS * MB_DH, lanes)

    def page_spec(r):
        return pl.BlockSpec((1, page * MB_HEADS, MB_DH),
                            lambda b, s, pt: (pt[b, s * PAGES_PER_STEP + r], 0, 0))

    per_batch = lambda shape: pl.BlockSpec((1,) + shape, lambda b, s, pt: (b,) + (0,) * len(shape))
    const = lambda shape: pl.BlockSpec(shape, lambda b, s, pt: (0,) * len(shape))

    p, p_new, l = pl.pallas_call(
        functools.partial(_moba_sample_scores_kernel, page=page, past=past, t_new=t_new),
        grid_spec=pltpu.PrefetchScalarGridSpec(
            num_scalar_prefetch=1,
            grid=(batch, n_steps),
            in_specs=[per_batch((MB_HEADS * MB_DH, lanes))] + [page_spec(r) for r in range(PAGES_PER_STEP)]
                     + [per_batch((t_new, MB_HEADS * MB_DH)), const(cached_b.shape), const(new_b.shape)],
            out_specs=[per_batch((past, lanes)), per_batch((t_new, lanes)), per_batch((1, lanes))],
            scratch_shapes=[pltpu.VMEM((past, lanes), F32), pltpu.VMEM((past // MB_BLOCK, lanes), F32),
                            pltpu.VMEM((past // MB_BLOCK, lanes), F32)]),
        out_shape=[jax.ShapeDtypeStruct((batch, past, lanes), BF16),
                   jax.ShapeDtypeStruct((batch, t_new, lanes), F32),
                   jax.ShapeDtypeStruct((batch, 1, lanes), F32)],
        compiler_params=_cparams("parallel", "arbitrary"),
        name="moba_sample_scores",
    )(page_table, qbd, *([ck] * PAGES_PER_STEP), ak16.reshape(batch, t_new, -1), cached_b, new_b)

    y = pl.pallas_call(
        functools.partial(_moba_sample_values_kernel, page=page, t_new=t_new),
        grid_spec=pltpu.PrefetchScalarGridSpec(
            num_scalar_prefetch=1,
            grid=(batch, n_steps),
            in_specs=[pl.BlockSpec((1, PAGES_PER_STEP * page, lanes), lambda b, s, pt: (b, s, 0))]
                     + [page_spec(r) for r in range(PAGES_PER_STEP)]
                     + [per_batch((t_new, lanes)), per_batch((t_new, MB_HEADS * MB_DH)), per_batch((lanes, 1))],
            out_specs=per_batch((t_new, MB_HEADS * MB_DH)),
            scratch_shapes=[pltpu.VMEM((lanes, MB_HEADS * MB_DH), F32)]),
        out_shape=jax.ShapeDtypeStruct((batch, t_new, MB_HEADS * MB_DH), BF16),
        compiler_params=_cparams("parallel", "arbitrary"),
        name="moba_sample_values",
    )(page_table, p, *([cv] * PAGES_PER_STEP), p_new, av16.reshape(batch, t_new, -1),
      l.reshape(batch, lanes, 1))
    return y.reshape(batch * t_new, MB_HEADS * MB_DH)


def _memkv_kernel(mem_ref, g_ref, w_ref, kn_ref, mk_ref, mv_ref):
    xb = _rms(mem_ref[...], g_ref[...]).astype(BF16)
    width = X_HEADS * X_DH
    mk_ref[...] = _head_rms(_dot(xb, w_ref[:, :width]), kn_ref[...], X_DH)
    mv_ref[...] = _dot(xb, w_ref[:, width:])


def _memkv(mem, mem_norm_g, w_mem_kv, cross_kn_g):
    m = mem.shape[0]
    width = X_HEADS * X_DH
    return pl.pallas_call(
        _memkv_kernel,
        out_shape=[jax.ShapeDtypeStruct((m, width), F32), jax.ShapeDtypeStruct((m, width), F32)],
        compiler_params=pltpu.CompilerParams(vmem_limit_bytes=VMEM_LIMIT),
        name="memkv",
    )(mem, mem_norm_g, w_mem_kv, cross_kn_g)


def _cross_kernel(q_ref, mk_ref, mv_ref, o_ref, *, head_major_rows):
    scale = X_DH ** -0.5
    lane_tiles = X_DH // 128

    def head(ref, h):
        if not head_major_rows:
            return ref[0, :, h * X_DH:(h + 1) * X_DH].astype(BF16)
        n_mem = ref.shape[1] // (X_HEADS * lane_tiles)
        parts = [ref[0, pl.ds(h * lane_tiles + c, n_mem, stride=X_HEADS * lane_tiles), :]
                 for c in range(lane_tiles)]
        return jnp.concatenate(parts, axis=1).astype(BF16)

    for h in range(X_HEADS):
        cols = slice(h * X_DH, (h + 1) * X_DH)
        s = _dot_nt(q_ref[0, :, cols], head(mk_ref, h)) * scale
        p = jnp.exp(s - jnp.max(s, axis=1, keepdims=True))
        p = p / jnp.sum(p, axis=1, keepdims=True)
        o_ref[0, :, cols] = _dot(p.astype(BF16), head(mv_ref, h)).astype(BF16)


def _cross(cq, mk, mv, batch, tq):
    n = cq.shape[0]
    t = n // batch
    assert t % tq == 0
    nt = t // tq
    width = X_HEADS * X_DH
    head_major_rows = mk.ndim == 4
    if head_major_rows:
        mk, mv = (a.reshape(batch, -1, 128) for a in (mk, mv))
    q_spec = pl.BlockSpec((1, tq, width), lambda b, i: (b * nt + i, 0, 0))
    mem_spec = pl.BlockSpec((1,) + mk.shape[1:], lambda b, i: (b, 0, 0))
    y = pl.pallas_call(
        functools.partial(_cross_kernel, head_major_rows=head_major_rows),
        grid=(batch, nt),
        in_specs=[q_spec, mem_spec, mem_spec],
        out_specs=q_spec,
        out_shape=jax.ShapeDtypeStruct((batch * nt, tq, width), BF16),
        compiler_params=_cparams("parallel", "arbitrary"),
        name="cross",
    )(cq.reshape(batch * nt, tq, width), mk, mv)
    return y.reshape(n, width)


def _merge_kernel(x_ref, ym_ref, ya_ref, yc_ref, g_ref, wm_ref, wa_ref, wc_ref, wo_ref, o_ref):
    mixed = (g_ref[:, 0:1024] * _dot(ym_ref[...], wm_ref[...])
             + g_ref[:, 1024:2048] * _dot(ya_ref[...], wa_ref[...])
             + g_ref[:, 2048:3072] * _dot(yc_ref[...], wc_ref[...]))
    o_ref[...] = x_ref[...] + _dot(mixed.astype(BF16), wo_ref[...])


def _merge(x, y_m, y_a, y_c, g, w_m, w_a, w_c, w_o, tm=256):
    n = x.shape[0]
    tm = min(tm, n)
    assert n % tm == 0
    row = lambda w: pl.BlockSpec((tm, w), lambda i: (i, 0))
    return pl.pallas_call(
        _merge_kernel,
        grid=(n // tm,),
        in_specs=[row(1024), row(1024), row(1024), row(1024), row(3072)]
                 + [_resident((1024, 1024))] * 4,
        out_specs=row(1024),
        out_shape=jax.ShapeDtypeStruct((n, 1024), F32),
        compiler_params=_cparams("parallel"),
        name="merge",
    )(x, y_m, y_a, y_c, g, w_m, w_a, w_c, w_o)


def _swiglu_kernel(x_ref, g_ref, wgu_ref, wd_ref, o_ref):
    x = x_ref[...]
    gu = _dot(_rms(x, g_ref[...]).astype(BF16), wgu_ref[...])
    d_ff = wd_ref.shape[0]
    gt, up = gu[:, :d_ff], gu[:, d_ff:]
    o_ref[...] = x + _dot((gt * jax.nn.sigmoid(gt) * up).astype(BF16), wd_ref[...])


def _swiglu(x, norm2_g, w_gu, w_down, tm=256):
    n = x.shape[0]
    tm = min(tm, n)
    assert n % tm == 0
    row = pl.BlockSpec((tm, 1024), lambda i: (i, 0))
    return pl.pallas_call(
        _swiglu_kernel,
        grid=(n // tm,),
        in_specs=[row, _resident((1, 1024)), _resident(w_gu.shape), _resident(w_down.shape)],
        out_specs=row,
        out_shape=jax.ShapeDtypeStruct((n, 1024), F32),
        compiler_params=_cparams("parallel"),
        name="swiglu",
    )(x, norm2_g, w_gu, w_down)


def _pick_chunk(t):
    for c in (256, 128, 64):
        if t % c == 0:
            return c
    return t


def _layer(x3, ml_state, mem_k, mem_v, moba, w):
    batch, t, _ = x3.shape
    x = x3.reshape(batch * t, D_MODEL)
    (qk, v, o_sig, aq, ak32, ak16, av32, av16, av_t, cq, g, gif) = _inproj(
        x, w['norm1_g'], w['w_main'], w['w_if'], w['b_if'], w['moba_qn_g'], w['moba_kn_g'], w['cross_qn_g'])
    y_m, c_out, n_out, m_out = _mlstm(qk, v, o_sig, gif, ml_state[0], ml_state[1], ml_state[2],
                                      w['mlstm_hn_g'], batch, _pick_chunk(t))
    y_a = moba(aq, ak16, av16, av_t)
    y_c = _cross(cq, mem_k, mem_v, batch, min(t, 256))
    x1 = _merge(x, y_m, y_a, y_c, g, w['w_br_m'], w['w_br_a'], w['w_br_c'], w['w_out'])
    y = _swiglu(x1, w['norm2_g'], w['w_gu'], w['w_down'])
    return (y.reshape(batch, t, D_MODEL), ak32.reshape(batch, t, MB_HEADS, MB_DH),
            av32.reshape(batch, t, MB_HEADS, MB_DH), c_out, n_out, m_out.reshape(batch, ML_HEADS))


def kernel(x_prompt, x_sample, mem_prompt, cache_k, cache_v, cache_mem_k, cache_mem_v, state_C, state_n, state_m, page_table, norm1_g, w_in, b_ig, b_fg, mlstm_hn_g, moba_qn_g, moba_kn_g, rel_bias, cross_qn_g, cross_kn_g, mem_norm_g, w_mem_kv, w_br_m, w_br_a, w_br_c, w_out, norm2_g, w_gu, w_down):
    c_if = 2 * ML_HEADS * ML_DQK + 2 * ML_HEADS * ML_DV
    row = lambda a: a.reshape(1, -1)
    w = {
        'norm1_g': row(norm1_g),
        'w_main': jnp.concatenate([w_in[:, :c_if], w_in[:, c_if + 2 * ML_HEADS:]], axis=1).astype(BF16),
        'w_if': w_in[:, c_if:c_if + 2 * ML_HEADS],
        'b_if': row(jnp.concatenate([b_ig, b_fg])),
        'mlstm_hn_g': row(mlstm_hn_g), 'moba_qn_g': row(moba_qn_g), 'moba_kn_g': row(moba_kn_g),
        'cross_qn_g': row(cross_qn_g),
        'w_br_m': w_br_m.astype(BF16), 'w_br_a': w_br_a.astype(BF16), 'w_br_c': w_br_c.astype(BF16),
        'w_out': w_out.astype(BF16), 'norm2_g': row(norm2_g),
        'w_gu': w_gu.astype(BF16), 'w_down': w_down.astype(BF16),
    }
    assert w['w_main'].shape[1] == N_SEG * 1024

    bp, sp, _ = x_prompt.shape
    assert bp == 1
    zero_state = (jnp.zeros((bp, ML_HEADS, ML_DQK, ML_DV), F32), jnp.zeros((bp, ML_HEADS, ML_DQK), F32),
                  jnp.zeros((bp, ML_HEADS), F32))
    n_mem = mem_prompt.shape[1]
    mem_k_p, mem_v_p = _memkv(mem_prompt.reshape(bp * n_mem, D_MODEL), row(mem_norm_g),
                              w_mem_kv.astype(BF16), row(cross_kn_g))
    mem_k_p = mem_k_p.reshape(bp, n_mem, X_HEADS * X_DH)
    mem_v_p = mem_v_p.reshape(bp, n_mem, X_HEADS * X_DH)
    bias_tiles = _prompt_bias(rel_bias)
    y_p, k_p, v_p, c_p, n_p, m_p = _layer(
        x_prompt, zero_state, mem_k_p, mem_v_p,
        lambda aq, ak, av, av_t: _moba_prompt(aq, ak, av_t, bias_tiles), w)

    bs, ts, _ = x_sample.shape
    rb_lanes = jnp.repeat(rel_bias, ts, axis=1)
    cached_b, new_b = _sample_bias(rb_lanes, cache_k.shape[1], ts)
    y_s, k_s, v_s, c_s, n_s, m_s = _layer(
        x_sample, (state_C, state_n, state_m),
        cache_mem_k, cache_mem_v,
        lambda aq, ak, av, av_t: _moba_sample(aq, ak, av, cache_k, cache_v, page_table, cached_b, new_b, bs), w)

    return (y_p, y_s, k_p, v_p, mem_k_p.reshape(bp, n_mem, X_HEADS, X_DH),
            mem_v_p.reshape(bp, n_mem, X_HEADS, X_DH), c_p, n_p, m_p, k_s, v_s, c_s, n_s, m_s)
```

```python
import functools
import math

import jax
import jax.numpy as jnp
from jax import lax
from jax.experimental import pallas as pl
from jax.experimental.pallas import tpu as pltpu

F32 = jnp.float32
BF16 = jnp.bfloat16

EPS = 1e-6
D_MODEL = 1024
ML_HEADS, ML_DQK, ML_DV = 4, 128, 256
GATE_CAP = 15.0
MB_HEADS, MB_DH, MB_BLOCK, MB_TOPK = 8, 128, 256, 3
X_HEADS, X_DH = 4, 256
REL_BUCKETS, REL_MAX_DIST = 32, 1024
NEG = -1e30
LOG2E = math.log2(math.e)
NEAR_BLOCKS = 5
N_SEG = 10

VMEM_LIMIT = 56 * 1024 * 1024


def _cparams(*sem):
    return pltpu.CompilerParams(dimension_semantics=sem, vmem_limit_bytes=VMEM_LIMIT)


def _resident(shape):
    nd = len(shape)
    return pl.BlockSpec(shape, lambda *_: (0,) * nd, pipeline_mode=pl.Buffered(1))


def _rms(x, g):
    return x * lax.rsqrt(jnp.mean(x * x, axis=-1, keepdims=True) + EPS) * g


def _head_rms(x, g, width):
    parts = [_rms(x[:, s:s + width], g) for s in range(0, x.shape[1], width)]
    return jnp.concatenate(parts, axis=1)


def _dot(a, b):
    return jnp.dot(a, b, preferred_element_type=F32)


def _dot_nt(a, b):
    return lax.dot_general(a, b, (((1,), (1,)), ((), ())), preferred_element_type=F32)


def _dot_tn(a, b):
    return lax.dot_general(a, b, (((0,), (0,)), ((), ())), preferred_element_type=F32)


def _split_hi_lo(x):
    hi = x.astype(BF16)
    lo = (x - hi.astype(F32)).astype(BF16)
    return hi, lo


def _rel_bucket(dist):
    max_exact = REL_BUCKETS // 2
    d = jnp.maximum(dist, 1).astype(F32)
    large = max_exact + (jnp.log(d / max_exact) / math.log(REL_MAX_DIST / max_exact)
                         * (REL_BUCKETS - max_exact)).astype(jnp.int32)
    large = jnp.minimum(large, REL_BUCKETS - 1)
    return jnp.where(dist < max_exact, dist, large)


def _inproj_kernel(x_ref, g1_ref, w_ref, wif_ref, bif_ref, qn_ref, kn_ref, cqn_ref,
                   qk_ref, v_ref, o_ref, aq_ref, ak32_ref, ak16_ref, av32_ref, av16_ref, avt_ref,
                   cq_ref, g_ref, gif_ref):
    xn = _rms(x_ref[...], g1_ref[...])
    xb = xn.astype(BF16)

    def seg(s):
        return _dot(xb, w_ref[:, s * 1024:(s + 1) * 1024])

    lane = lax.broadcasted_iota(jnp.int32, (1, 1024), 1)
    kscale = jnp.where(lane >= ML_HEADS * ML_DQK, ML_DQK ** -0.5, 1.0).astype(F32)
    qk_ref[...] = (seg(0) * kscale).astype(BF16)
    v_ref[...] = seg(1).astype(BF16)
    o_ref[...] = jax.nn.sigmoid(seg(2))
    aq_ref[...] = _head_rms(seg(3), qn_ref[...], MB_DH).astype(BF16)
    ak = _head_rms(seg(4), kn_ref[...], MB_DH)
    ak32_ref[...] = ak
    ak16_ref[...] = ak.astype(BF16)
    av = seg(5)
    av32_ref[...] = av
    av16_ref[...] = av.astype(BF16)
    avt_ref[0] = av.T.astype(BF16)
    cq_ref[...] = _head_rms(seg(6), cqn_ref[...], X_DH).astype(BF16)
    for s in range(3):
        g_ref[:, s * 1024:(s + 1) * 1024] = jax.nn.sigmoid(seg(7 + s))

    x_lo = (xn - xb.astype(F32)).astype(BF16)
    w_hi, w_lo = _split_hi_lo(wif_ref[...])
    z = _dot(xb, w_hi) + (_dot(x_lo, w_hi) + _dot(xb, w_lo)) + bif_ref[...]
    capped = GATE_CAP * jnp.tanh(z / GATE_CAP)
    log_sig = -(jnp.maximum(-capped, 0.0) + jnp.log1p(jnp.exp(-jnp.abs(capped))))
    col = lax.broadcasted_iota(jnp.int32, z.shape, 1)
    gif_ref[...] = jnp.where(col < ML_HEADS, capped, log_sig)


def _inproj(x, norm1_g, w_main, w_if, b_if, qn_g, kn_g, cqn_g, tm=256):
    n = x.shape[0]
    tm = min(tm, n)
    assert n % tm == 0
    row = lambda w: pl.BlockSpec((tm, w), lambda i: (i, 0))
    shapes = [(1024, BF16), (1024, BF16), (1024, F32), (1024, BF16), (1024, F32), (1024, BF16),
              (1024, F32), (1024, BF16), None, (1024, BF16), (3072, F32), (2 * ML_HEADS, F32)]
    out_specs = [pl.BlockSpec((1, 1024, tm), lambda i: (i, 0, 0)) if s is None else row(s[0]) for s in shapes]
    out_shape = [jax.ShapeDtypeStruct((n // tm, 1024, tm), BF16) if s is None
                 else jax.ShapeDtypeStruct((n, s[0]), s[1]) for s in shapes]
    return pl.pallas_call(
        _inproj_kernel,
        grid=(n // tm,),
        in_specs=[row(1024), _resident((1, 1024)), _resident(w_main.shape), _resident(w_if.shape),
                  _resident((1, 2 * ML_HEADS)), _resident((1, MB_DH)), _resident((1, MB_DH)),
                  _resident((1, X_DH))],
        out_specs=out_specs,
        out_shape=out_shape,
        compiler_params=_cparams("parallel"),
        name="inproj",
    )(x, norm1_g, w_main, w_if, b_if, qn_g, kn_g, cqn_g)


def _mlstm_kernel(qk_ref, v_ref, o_ref, gc_ref, gr_ref, c0_ref, n0_ref, m0_ref, hn_ref,
                  y_ref, cout_ref, nout_ref, mout_ref, c_s, n_s, m_s, *, chunk, n_chunks):
    L = chunk
    c = pl.program_id(1)

    @pl.when(c == 0)
    def _():
        c_s[...] = c0_ref[0]
        n_s[...] = n0_ref[0]
        m_s[...] = m0_ref[0]

    gc = gc_ref[0]
    gr = gr_ref[0]
    t_idx = lax.broadcasted_iota(jnp.int32, (L, L), 0)
    s_idx = lax.broadcasted_iota(jnp.int32, (L, L), 1)
    causal = s_idx <= t_idx

    for h in range(ML_HEADS):
        ig_row, lf_row = gr[h:h + 1, :], gr[ML_HEADS + h:ML_HEADS + h + 1, :]
        ig_col, lf_col = gc[:, h:h + 1], gc[:, ML_HEADS + h:ML_HEADS + h + 1]
        b_col = jnp.sum(jnp.where(causal, lf_row, 0.0), axis=1, keepdims=True)
        b_row = jnp.sum(jnp.where(t_idx <= s_idx, lf_col, 0.0), axis=0, keepdims=True)
        m_prev = m_s[:, h:h + 1]

        log_d = jnp.where(causal, b_col - b_row + ig_row, -jnp.inf)
        m_inter = b_col + m_prev
        m_t = jnp.maximum(m_inter, jnp.max(log_d, axis=1, keepdims=True))
        w_intra = jnp.exp(log_d - m_t)
        w_inter = jnp.exp(m_inter - m_t)

        q = qk_ref[0, :, h * ML_DQK:(h + 1) * ML_DQK]
        k = qk_ref[0, :, (ML_HEADS + h) * ML_DQK:(ML_HEADS + h + 1) * ML_DQK]
        v = v_ref[0, :, h * ML_DV:(h + 1) * ML_DV]
        c_h = c_s[h]
        n_h = n_s[h:h + 1, :]

        s = _dot_nt(q, k) * w_intra
        num = w_inter * _dot(q, c_h.astype(BF16)) + _dot(s.astype(BF16), v)
        qn = jnp.sum(q.astype(F32) * n_h, axis=1, keepdims=True)
        den = w_inter * qn + jnp.sum(s, axis=1, keepdims=True)
        hh = num / jnp.maximum(jnp.abs(den), jnp.exp(-m_t))

        b_last = b_col[L - 1:L, :]
        m_new = jnp.maximum(b_last + m_prev,
                            jnp.max(b_last - b_row + ig_row, axis=1, keepdims=True))
        w_col = jnp.exp(b_last - b_col + ig_col - m_new)
        decay = jnp.exp(b_last + m_prev - m_new)
        kw = k.astype(F32) * w_col
        c_s[h] = decay * c_h + _dot_tn(kw.astype(BF16), v)
        n_s[h:h + 1, :] = decay * n_h + jnp.sum(kw, axis=0, keepdims=True)
        m_s[:, h:h + 1] = m_new

        hn = _rms(hh, hn_ref[:, h * ML_DV:(h + 1) * ML_DV])
        y_ref[0, :, h * ML_DV:(h + 1) * ML_DV] = (o_ref[0, :, h * ML_DV:(h + 1) * ML_DV] * hn).astype(BF16)

    @pl.when(c == n_chunks - 1)
    def _():
        cout_ref[0] = c_s[...]
        nout_ref[0] = n_s[...]
        mout_ref[0] = m_s[...]


def _mlstm(qk, v, o_sig, gif, c0, n0, m0, hn_g, batch, chunk):
    n = qk.shape[0]
    t = n // batch
    nc = t // chunk
    chunked = lambda a: a.reshape(batch * nc, chunk, a.shape[-1])
    gc = chunked(gif)
    gr = gc.transpose(0, 2, 1)
    row = lambda w: pl.BlockSpec((1, chunk, w), lambda b, c: (b * nc + c, 0, 0))
    state = lambda shape: pl.BlockSpec((1,) + shape, lambda b, c: (b,) + (0,) * len(shape))
    y, c_out, n_out, m_out = pl.pallas_call(
        functools.partial(_mlstm_kernel, chunk=chunk, n_chunks=nc),
        grid=(batch, nc),
        in_specs=[row(1024), row(1024), row(1024), row(2 * ML_HEADS),
                  pl.BlockSpec((1, 2 * ML_HEADS, chunk), lambda b, c: (b * nc + c, 0, 0)),
                  state((ML_HEADS, ML_DQK, ML_DV)), state((ML_HEADS, ML_DQK)), state((1, ML_HEADS)),
                  _resident((1, ML_HEADS * ML_DV))],
        out_specs=[row(1024), state((ML_HEADS, ML_DQK, ML_DV)), state((ML_HEADS, ML_DQK)),
                   state((1, ML_HEADS))],
        out_shape=[jax.ShapeDtypeStruct((batch * nc, chunk, 1024), BF16),
                   jax.ShapeDtypeStruct((batch, ML_HEADS, ML_DQK, ML_DV), F32),
                   jax.ShapeDtypeStruct((batch, ML_HEADS, ML_DQK), F32),
                   jax.ShapeDtypeStruct((batch, 1, ML_HEADS), F32)],
        scratch_shapes=[pltpu.VMEM((ML_HEADS, ML_DQK, ML_DV), F32), pltpu.VMEM((ML_HEADS, ML_DQK), F32),
                        pltpu.VMEM((1, ML_HEADS), F32)],
        compiler_params=_cparams("parallel", "arbitrary"),
        name="mlstm",
    )(chunked(qk), chunked(v), chunked(o_sig), gc, gr, c0, n0, m0.reshape(batch, 1, ML_HEADS), hn_g)
    return y.reshape(n, 1024), c_out, n_out, m_out


def _bucket_lookup(bucket, table_row):
    val = jnp.zeros(bucket.shape, F32)
    for b in range(REL_BUCKETS):
        val = jnp.where(bucket == b, table_row(b), val)
    return val


def _prompt_bias_kernel(rb_ref, o_ref):
    h = pl.program_id(0)
    kj = lax.broadcasted_iota(jnp.int32, (MB_BLOCK, MB_BLOCK), 0)
    qi = lax.broadcasted_iota(jnp.int32, (MB_BLOCK, MB_BLOCK), 1)
    for d in range(NEAR_BLOCKS + 1):
        bucket = _rel_bucket(jnp.maximum(d * MB_BLOCK + qi - kj, 0))
        bias = _bucket_lookup(bucket, lambda b: rb_ref[b, h]) * LOG2E
        o_ref[0, d] = jnp.where(kj <= qi, bias, NEG) if d == 0 else bias


def _prompt_bias(rel_bias):
    return pl.pallas_call(
        _prompt_bias_kernel,
        grid=(MB_HEADS,),
        in_specs=[pl.BlockSpec(memory_space=pltpu.SMEM)],
        out_specs=pl.BlockSpec((1, NEAR_BLOCKS + 1, MB_BLOCK, MB_BLOCK), lambda h: (h, 0, 0, 0)),
        out_shape=jax.ShapeDtypeStruct((MB_HEADS, NEAR_BLOCKS + 1, MB_BLOCK, MB_BLOCK), F32),
        compiler_params=_cparams("parallel"),
        name="prompt_bias",
    )(rel_bias)


N_NEAR = (NEAR_BLOCKS - 1) * MB_BLOCK


def _sample_bias_kernel(rbl_ref, cached_ref, new_ref, *, page, t_new):
    lanes = cached_ref.shape[1]
    lookup = lambda bucket: _bucket_lookup(bucket, lambda b: rbl_ref[b:b + 1, :])
    r = lax.broadcasted_iota(jnp.int32, cached_ref.shape, 0)
    tq = lax.broadcasted_iota(jnp.int32, cached_ref.shape, 1) % t_new
    cached_ref[...] = lookup(_rel_bucket(jnp.where(r < page, N_NEAR + 1, tq + N_NEAR + page - r)))
    r = lax.broadcasted_iota(jnp.int32, (t_new, lanes), 0)
    tq = lax.broadcasted_iota(jnp.int32, (t_new, lanes), 1) % t_new
    new_ref[...] = lookup(_rel_bucket(jnp.maximum(tq - r, 0)))


def _sample_bias(rb_lanes, page, t_new):
    lanes = rb_lanes.shape[1]
    return pl.pallas_call(
        functools.partial(_sample_bias_kernel, page=page, t_new=t_new),
        out_shape=[jax.ShapeDtypeStruct((page + N_NEAR, lanes), F32), jax.ShapeDtypeStruct((t_new, lanes), F32)],
        name="sample_bias",
    )(rb_lanes)


MOBA_GROUP = 4
MOBA_RUNS = (8, 4, 2, 1)
ONES_ROWS = 16


def _run_lengths(n_groups, fn, carry):
    done = 0
    for idx, run in enumerate(MOBA_RUNS):
        trips = (n_groups - done) // run if idx == 0 else ((n_groups - done) // run) % 2
        first = done

        def body(r, c, first=first, run=run):
            return fn((first + r * run) * MOBA_GROUP, run * MOBA_GROUP, c)
        carry = lax.fori_loop(0, trips, body, carry)
        done = done + trips * run
    return carry


def _moba_prompt_kernel(q_ref, k_ref, vt_ref, bias_ref, o_ref, means_s, t_s, *, n_blocks):
    i = pl.program_id(1)
    kblk = lambda j: k_ref[pl.ds(pl.multiple_of(j * MB_BLOCK, MB_BLOCK), MB_BLOCK), :]

    @pl.when(i == 0)
    def _():
        def body(j, carry):
            means_s[pl.ds(j, 1), :] = jnp.sum(kblk(j).astype(F32), axis=0, keepdims=True) * (1.0 / MB_BLOCK)
            return carry
        lax.fori_loop(0, n_blocks, body, 0)

    q = q_ref[...]
    c = MB_DH ** -0.5 * LOG2E
    m_hi, m_lo = _split_hi_lo(means_s[...])
    gate = _dot_nt(m_hi, q) + _dot_nt(m_lo, q)
    blk = lax.broadcasted_iota(jnp.int32, gate.shape, 0)
    gate = jnp.where(blk < i, gate, NEG)
    sel = []
    for _ in range(MB_TOPK):
        mx = jnp.max(gate, axis=0, keepdims=True)
        idx = jnp.min(jnp.where(gate == mx, blk, n_blocks), axis=0, keepdims=True)
        sel.append(jnp.where(idx < i, idx, -1))
        gate = jnp.where(blk == idx, -jnp.inf, gate)

    n_groups = (i + MOBA_GROUP) // MOBA_GROUP
    clamp = lambda jt: jnp.minimum(jt, n_blocks - 1)

    def logits_and_max(first, count, m):
        for u in range(count):
            jt = first + u
            chosen = (sel[0] == jt) | (sel[1] == jt) | (sel[2] == jt) | (jt == i)
            bias = bias_ref[0, jnp.clip(i - jt, 0, NEAR_BLOCKS)]
            t = jnp.where(chosen, _dot_nt(kblk(clamp(jt)), q) * c + bias, NEG)
            t_s[jt] = t
            m = jnp.maximum(m, jnp.max(t, axis=0, keepdims=True))
        return m

    m = _run_lengths(n_groups, logits_and_max, jnp.full((1, MB_BLOCK), NEG, F32))

    ones = jnp.ones((ONES_ROWS, MB_BLOCK), BF16)

    def accumulate(first, count, acc):
        for u in range(count):
            jt = first + u
            p = jnp.exp2((t_s[jt] - m).astype(BF16))
            acc = acc + _dot(jnp.concatenate([vt_ref[clamp(jt)], ones], axis=0), p)
        return acc

    acc = _run_lengths(n_groups, accumulate, jnp.zeros((MB_DH + ONES_ROWS, MB_BLOCK), F32))
    o_ref[...] = (acc[:MB_DH] / acc[MB_DH:MB_DH + 1]).T.astype(BF16)


def _moba_prompt(aq, ak, av_t, bias_tiles):
    n = aq.shape[0]
    assert n % MB_BLOCK == 0
    nb = n // MB_BLOCK
    assert av_t.shape == (nb, MB_HEADS * MB_DH, MB_BLOCK)
    n_tiles = -(-nb // MOBA_GROUP) * MOBA_GROUP
    return pl.pallas_call(
        functools.partial(_moba_prompt_kernel, n_blocks=nb),
        grid=(MB_HEADS, nb),
        in_specs=[pl.BlockSpec((MB_BLOCK, MB_DH), lambda h, i: (i, h)),
                  pl.BlockSpec((n, MB_DH), lambda h, i: (0, h)),
                  pl.BlockSpec((nb, MB_DH, MB_BLOCK), lambda h, i: (0, h, 0)),
                  pl.BlockSpec((1, NEAR_BLOCKS + 1, MB_BLOCK, MB_BLOCK), lambda h, i: (h, 0, 0, 0))],
        out_specs=pl.BlockSpec((MB_BLOCK, MB_DH), lambda h, i: (i, h)),
        out_shape=jax.ShapeDtypeStruct((n, MB_HEADS * MB_DH), BF16),
        scratch_shapes=[pltpu.VMEM((nb, MB_DH), F32), pltpu.VMEM((n_tiles, MB_BLOCK, MB_BLOCK), F32)],
        compiler_params=_cparams("parallel", "arbitrary"),
        name="moba_prompt",
    )(aq, ak, av_t, bias_tiles)


PAGES_PER_STEP = 8
PAGE_RING = 3


def _page_ring_step(pt_ref, cache_hbm, buf, sem, *, batch, n_steps):
    lin = pl.program_id(0) * n_steps + pl.program_id(1)
    n_lin = batch * n_steps

    def copies(l):
        b, s, slot = l // n_steps, l % n_steps, l % PAGE_RING
        return [pltpu.make_async_copy(cache_hbm.at[pt_ref[b, s * PAGES_PER_STEP + r]],
                                      buf.at[slot, r], sem.at[slot, r]) for r in range(PAGES_PER_STEP)]

    def start(l):
        for r, cp in enumerate(copies(l)):
            cp.start(priority=r % 2)

    @pl.when(lin == 0)
    def _():
        for l in range(min(PAGE_RING - 1, n_lin)):
            start(jnp.int32(l))

    @pl.when(lin + PAGE_RING - 1 < n_lin)
    def _():
        start(lin + PAGE_RING - 1)

    for cp in copies(lin):
        cp.wait()
    return lin % PAGE_RING


def _page_heads_on_lanes(buf, slot, r, page):
    heads = [buf[slot, r, pl.ds(h, page, stride=MB_HEADS), :] for h in range(MB_HEADS)]
    return jnp.concatenate(heads, axis=1).astype(BF16)


def _moba_sample_scores_kernel(pt_ref, qbd_ref, k_hbm, knew_ref, bias_ref, newb_ref, p_ref, pnew_ref, l_ref,
                               t_s, gate_s, bmax_s, kbuf, ksem, *, page, past, t_new, batch):
    step = pl.program_id(1)
    n_steps = pl.num_programs(1)
    slot = _page_ring_step(pt_ref, k_hbm, kbuf, ksem, batch=batch, n_steps=past // page // PAGES_PER_STEP)
    nbp = past // MB_BLOCK
    ppb = MB_BLOCK // page
    n_far_pages = (past - N_NEAR) // page
    lanes = MB_HEADS * t_new
    scale = MB_DH ** -0.5
    qbd = qbd_ref[0]

    for b in range(PAGES_PER_STEP // ppb):
        gate, bmax = 0.0, NEG
        for r in range(b * ppb, (b + 1) * ppb):
            pg = step * PAGES_PER_STEP + r
            s = _dot(_page_heads_on_lanes(kbuf, slot, r, page), qbd)
            brow = pl.multiple_of(jnp.maximum(pg - n_far_pages + 1, 0) * page, page)
            t = s * scale + bias_ref[pl.ds(brow, page), :]
            t_s[pl.ds(pl.multiple_of(pg * page, page), page), :] = t
            gate = gate + jnp.sum(s, axis=0, keepdims=True)
            bmax = jnp.maximum(bmax, jnp.max(t, axis=0, keepdims=True))
        blk = step * (PAGES_PER_STEP // ppb) + b
        gate_s[pl.ds(blk, 1), :] = gate * (1.0 / MB_BLOCK)
        bmax_s[pl.ds(blk, 1), :] = bmax

    @pl.when(step == n_steps - 1)
    def _():
        gate = gate_s[...]
        bidx = lax.broadcasted_iota(jnp.int32, gate.shape, 0)
        chosen = jnp.zeros(gate.shape, jnp.bool_)
        for _ in range(MB_TOPK):
            mx = jnp.max(gate, axis=0, keepdims=True)
            idx = jnp.min(jnp.where(gate == mx, bidx, nbp), axis=0, keepdims=True)
            chosen = chosen | ((bidx == idx) & (mx > -jnp.inf))
            gate = jnp.where(bidx == idx, -jnp.inf, gate)

        tk = lax.broadcasted_iota(jnp.int32, (t_new, lanes), 0)
        tq = lax.broadcasted_iota(jnp.int32, (t_new, lanes), 1) % t_new
        t_new_keys = jnp.where(tk <= tq, _dot(knew_ref[0], qbd) * scale + newb_ref[...], NEG)

        m = jnp.maximum(jnp.max(t_new_keys, axis=0, keepdims=True),
                        jnp.max(jnp.where(chosen, bmax_s[...], NEG), axis=0, keepdims=True))
        e_new = jnp.exp(t_new_keys - m)
        pnew_ref[0] = e_new
        gate_s[...] = jnp.where(chosen, 0.0, NEG)

        def exp_block(j, l):
            rows = pl.ds(pl.multiple_of(j * MB_BLOCK, MB_BLOCK), MB_BLOCK)
            e = jnp.exp(jnp.minimum(t_s[rows, :] - m, 0.0) + gate_s[pl.ds(j, 1), :])
            p_ref[0, rows, :] = e.astype(BF16)
            return l + jnp.sum(e, axis=0, keepdims=True)

        l_ref[0] = lax.fori_loop(0, nbp, exp_block, jnp.sum(e_new, axis=0, keepdims=True))


def _moba_sample_values_kernel(pt_ref, p_ref, v_hbm, pnew_ref, vnew_ref, l_ref, o_ref, acc_s, vbuf, vsem,
                               *, page, t_new, batch, n_steps):
    step = pl.program_id(1)
    slot = _page_ring_step(pt_ref, v_hbm, vbuf, vsem, batch=batch, n_steps=n_steps)

    @pl.when(step == 0)
    def _():
        acc_s[...] = jnp.zeros_like(acc_s)

    for r in range(PAGES_PER_STEP):
        acc_s[...] += _dot_tn(p_ref[0, r * page:(r + 1) * page, :], _page_heads_on_lanes(vbuf, slot, r, page))

    @pl.when(step == pl.num_programs(1) - 1)
    def _():
        acc = acc_s[...] + _dot_tn(pnew_ref[0].astype(BF16), vnew_ref[0])
        acc = acc / l_ref[0]
        parts = [acc[h * t_new:(h + 1) * t_new, h * MB_DH:(h + 1) * MB_DH] for h in range(MB_HEADS)]
        o_ref[0] = jnp.concatenate(parts, axis=1).astype(BF16)


def _moba_sample(aq, ak16, av16, cache_k, cache_v, page_table, cached_b, new_b, batch):
    t_new = aq.shape[0] // batch
    n_pool, page = cache_k.shape[:2]
    n_pages = page_table.shape[1]
    past = n_pages * page
    lanes = MB_HEADS * t_new
    assert MB_BLOCK % page == 0 and PAGES_PER_STEP % (MB_BLOCK // page) == 0
    assert n_pages % PAGES_PER_STEP == 0 and past % MB_BLOCK == 0 and past >= N_NEAR
    n_steps = n_pages // PAGES_PER_STEP
    ck = cache_k.reshape(n_pool, page * MB_HEADS, MB_DH)
    cv = cache_v.reshape(n_pool, page * MB_HEADS, MB_DH)
    q4 = aq.reshape(batch, t_new, MB_HEADS, MB_DH)
    qbd = jnp.einsum('bthd,hg->bhdgt', q4, jnp.eye(MB_HEADS, dtype=aq.dtype)).reshape(
        batch, MB_HEADS * MB_DH, lanes)

    per_batch = lambda shape: pl.BlockSpec((1,) + shape, lambda b, s, pt: (b,) + (0,) * len(shape))
    const = lambda shape: pl.BlockSpec(shape, lambda b, s, pt: (0,) * len(shape))
    in_hbm = pl.BlockSpec(memory_space=pl.ANY)
    ring_scratch = [pltpu.VMEM((PAGE_RING, PAGES_PER_STEP, page * MB_HEADS, MB_DH), F32),
                    pltpu.SemaphoreType.DMA((PAGE_RING, PAGES_PER_STEP))]

    p, p_new, l = pl.pallas_call(
        functools.partial(_moba_sample_scores_kernel, page=page, past=past, t_new=t_new, batch=batch),
        grid_spec=pltpu.PrefetchScalarGridSpec(
            num_scalar_prefetch=1,
            grid=(batch, n_steps),
            in_specs=[per_batch((MB_HEADS * MB_DH, lanes)), in_hbm, per_batch((t_new, MB_HEADS * MB_DH)),
                      const(cached_b.shape), const(new_b.shape)],
            out_specs=[per_batch((past, lanes)), per_batch((t_new, lanes)), per_batch((1, lanes))],
            scratch_shapes=[pltpu.VMEM((past, lanes), F32), pltpu.VMEM((past // MB_BLOCK, lanes), F32),
                            pltpu.VMEM((past // MB_BLOCK, lanes), F32)] + ring_scratch),
        out_shape=[jax.ShapeDtypeStruct((batch, past, lanes), BF16),
                   jax.ShapeDtypeStruct((batch, t_new, lanes), F32),
                   jax.ShapeDtypeStruct((batch, 1, lanes), F32)],
        compiler_params=_cparams("arbitrary", "arbitrary"),
        name="moba_sample_scores",
    )(page_table, qbd, ck, ak16.reshape(batch, t_new, -1), cached_b, new_b)

    y = pl.pallas_call(
        functools.partial(_moba_sample_values_kernel, page=page, t_new=t_new, batch=batch, n_steps=n_steps),
        grid_spec=pltpu.PrefetchScalarGridSpec(
            num_scalar_prefetch=1,
            grid=(batch, n_steps),
            in_specs=[pl.BlockSpec((1, PAGES_PER_STEP * page, lanes), lambda b, s, pt: (b, s, 0)), in_hbm,
                      per_batch((t_new, lanes)), per_batch((t_new, MB_HEADS * MB_DH)), per_batch((lanes, 1))],
            out_specs=per_batch((t_new, MB_HEADS * MB_DH)),
            scratch_shapes=[pltpu.VMEM((lanes, MB_HEADS * MB_DH), F32)] + ring_scratch),
        out_shape=jax.ShapeDtypeStruct((batch, t_new, MB_HEADS * MB_DH), BF16),
        compiler_params=_cparams("arbitrary", "arbitrary"),
        name="moba_sample_values",
    )(page_table, p, cv, p_new, av16.reshape(batch, t_new, -1), l.reshape(batch, lanes, 1))
    return y.reshape(batch * t_new, MB_HEADS * MB_DH)


def _memkv_kernel(mem_ref, g_ref, w_ref, kn_ref, mk_ref, mv_ref):
    xb = _rms(mem_ref[...], g_ref[...]).astype(BF16)
    width = X_HEADS * X_DH
    mk_ref[...] = _head_rms(_dot(xb, w_ref[:, :width]), kn_ref[...], X_DH)
    mv_ref[...] = _dot(xb, w_ref[:, width:])


def _memkv(mem, mem_norm_g, w_mem_kv, cross_kn_g):
    m = mem.shape[0]
    width = X_HEADS * X_DH
    return pl.pallas_call(
        _memkv_kernel,
        out_shape=[jax.ShapeDtypeStruct((m, width), F32), jax.ShapeDtypeStruct((m, width), F32)],
        compiler_params=pltpu.CompilerParams(vmem_limit_bytes=VMEM_LIMIT),
        name="memkv",
    )(mem, mem_norm_g, w_mem_kv, cross_kn_g)


def _cross_kernel(q_ref, mk_ref, mv_ref, o_ref, *, head_major_rows):
    scale = X_DH ** -0.5
    lane_tiles = X_DH // 128

    def head(ref, h):
        if not head_major_rows:
            return ref[0, :, h * X_DH:(h + 1) * X_DH].astype(BF16)
        n_mem = ref.shape[1] // (X_HEADS * lane_tiles)
        parts = [ref[0, pl.ds(h * lane_tiles + c, n_mem, stride=X_HEADS * lane_tiles), :]
                 for c in range(lane_tiles)]
        return jnp.concatenate(parts, axis=1).astype(BF16)

    for h in range(X_HEADS):
        cols = slice(h * X_DH, (h + 1) * X_DH)
        s = _dot_nt(q_ref[0, :, cols], head(mk_ref, h)) * scale
        p = jnp.exp(s - jnp.max(s, axis=1, keepdims=True))
        p = p / jnp.sum(p, axis=1, keepdims=True)
        o_ref[0, :, cols] = _dot(p.astype(BF16), head(mv_ref, h)).astype(BF16)


def _cross(cq, mk, mv, batch, tq):
    n = cq.shape[0]
    t = n // batch
    assert t % tq == 0
    nt = t // tq
    width = X_HEADS * X_DH
    head_major_rows = mk.ndim == 4
    if head_major_rows:
        mk, mv = (a.reshape(batch, -1, 128) for a in (mk, mv))
    q_spec = pl.BlockSpec((1, tq, width), lambda b, i: (b * nt + i, 0, 0))
    mem_spec = pl.BlockSpec((1,) + mk.shape[1:], lambda b, i: (b, 0, 0))
    y = pl.pallas_call(
        functools.partial(_cross_kernel, head_major_rows=head_major_rows),
        grid=(batch, nt),
        in_specs=[q_spec, mem_spec, mem_spec],
        out_specs=q_spec,
        out_shape=jax.ShapeDtypeStruct((batch * nt, tq, width), BF16),
        compiler_params=_cparams("parallel", "arbitrary"),
        name="cross",
    )(cq.reshape(batch * nt, tq, width), mk, mv)
    return y.reshape(n, width)


def _merge_kernel(x_ref, ym_ref, ya_ref, yc_ref, g_ref, wm_ref, wa_ref, wc_ref, wo_ref, o_ref):
    mixed = (g_ref[:, 0:1024] * _dot(ym_ref[...], wm_ref[...])
             + g_ref[:, 1024:2048] * _dot(ya_ref[...], wa_ref[...])
             + g_ref[:, 2048:3072] * _dot(yc_ref[...], wc_ref[...]))
    o_ref[...] = x_ref[...] + _dot(mixed.astype(BF16), wo_ref[...])


def _merge(x, y_m, y_a, y_c, g, w_m, w_a, w_c, w_o, tm=512):
    n = x.shape[0]
    tm = min(tm, n)
    assert n % tm == 0
    row = lambda w: pl.BlockSpec((tm, w), lambda i: (i, 0))
    return pl.pallas_call(
        _merge_kernel,
        grid=(n // tm,),
        in_specs=[row(1024), row(1024), row(1024), row(1024), row(3072)]
                 + [_resident((1024, 1024))] * 4,
        out_specs=row(1024),
        out_shape=jax.ShapeDtypeStruct((n, 1024), F32),
        compiler_params=_cparams("parallel"),
        name="merge",
    )(x, y_m, y_a, y_c, g, w_m, w_a, w_c, w_o)


def _swiglu_kernel(x_ref, g_ref, wgu_ref, wd_ref, o_ref):
    x = x_ref[...]
    gu = _dot(_rms(x, g_ref[...]).astype(BF16), wgu_ref[...])
    d_ff = wd_ref.shape[0]
    gt, up = gu[:, :d_ff], gu[:, d_ff:]
    o_ref[...] = x + _dot((gt * jax.nn.sigmoid(gt) * up).astype(BF16), wd_ref[...])


def _swiglu(x, norm2_g, w_gu, w_down, tm=512):
    n = x.shape[0]
    tm = min(tm, n)
    assert n % tm == 0
    row = pl.BlockSpec((tm, 1024), lambda i: (i, 0))
    return pl.pallas_call(
        _swiglu_kernel,
        grid=(n // tm,),
        in_specs=[row, _resident((1, 1024)), _resident(w_gu.shape), _resident(w_down.shape)],
        out_specs=row,
        out_shape=jax.ShapeDtypeStruct((n, 1024), F32),
        compiler_params=_cparams("parallel"),
        name="swiglu",
    )(x, norm2_g, w_gu, w_down)


def _pick_chunk(t):
    for c in (256, 128, 64):
        if t % c == 0:
            return c
    return t


def _layer(x3, ml_state, mem_k, mem_v, moba, w):
    batch, t, _ = x3.shape
    x = x3.reshape(batch * t, D_MODEL)
    (qk, v, o_sig, aq, ak32, ak16, av32, av16, av_t, cq, g, gif) = _inproj(
        x, w['norm1_g'], w['w_main'], w['w_if'], w['b_if'], w['moba_qn_g'], w['moba_kn_g'], w['cross_qn_g'])
    y_m, c_out, n_out, m_out = _mlstm(qk, v, o_sig, gif, ml_state[0], ml_state[1], ml_state[2],
                                      w['mlstm_hn_g'], batch, _pick_chunk(t))
    y_a = moba(aq, ak16, av16, av_t)
    y_c = _cross(cq, mem_k, mem_v, batch, min(t, 256))
    x1 = _merge(x, y_m, y_a, y_c, g, w['w_br_m'], w['w_br_a'], w['w_br_c'], w['w_out'])
    y = _swiglu(x1, w['norm2_g'], w['w_gu'], w['w_down'])
    return (y.reshape(batch, t, D_MODEL), ak32.reshape(batch, t, MB_HEADS, MB_DH),
            av32.reshape(batch, t, MB_HEADS, MB_DH), c_out, n_out, m_out.reshape(batch, ML_HEADS))


def kernel(x_prompt, x_sample, mem_prompt, cache_k, cache_v, cache_mem_k, cache_mem_v, state_C, state_n, state_m, page_table, norm1_g, w_in, b_ig, b_fg, mlstm_hn_g, moba_qn_g, moba_kn_g, rel_bias, cross_qn_g, cross_kn_g, mem_norm_g, w_mem_kv, w_br_m, w_br_a, w_br_c, w_out, norm2_g, w_gu, w_down):
    c_if = 2 * ML_HEADS * ML_DQK + 2 * ML_HEADS * ML_DV
    row = lambda a: a.reshape(1, -1)
    w = {
        'norm1_g': row(norm1_g),
        'w_main': jnp.concatenate([w_in[:, :c_if], w_in[:, c_if + 2 * ML_HEADS:]], axis=1).astype(BF16),
        'w_if': w_in[:, c_if:c_if + 2 * ML_HEADS],
        'b_if': row(jnp.concatenate([b_ig, b_fg])),
        'mlstm_hn_g': row(mlstm_hn_g), 'moba_qn_g': row(moba_qn_g), 'moba_kn_g': row(moba_kn_g),
        'cross_qn_g': row(cross_qn_g),
        'w_br_m': w_br_m.astype(BF16), 'w_br_a': w_br_a.astype(BF16), 'w_br_c': w_br_c.astype(BF16),
        'w_out': w_out.astype(BF16), 'norm2_g': row(norm2_g),
        'w_gu': w_gu.astype(BF16), 'w_down': w_down.astype(BF16),
    }
    assert w['w_main'].shape[1] == N_SEG * 1024

    bp, sp, _ = x_prompt.shape
    assert bp == 1
    zero_state = (jnp.zeros((bp, ML_HEADS, ML_DQK, ML_DV), F32), jnp.zeros((bp, ML_HEADS, ML_DQK), F32),
                  jnp.zeros((bp, ML_HEADS), F32))
    n_mem = mem_prompt.shape[1]
    mem_k_p, mem_v_p = _memkv(mem_prompt.reshape(bp * n_mem, D_MODEL), row(mem_norm_g),
                              w_mem_kv.astype(BF16), row(cross_kn_g))
    mem_k_p = mem_k_p.reshape(bp, n_mem, X_HEADS * X_DH)
    mem_v_p = mem_v_p.reshape(bp, n_mem, X_HEADS * X_DH)
    bias_tiles = _prompt_bias(rel_bias)
    y_p, k_p, v_p, c_p, n_p, m_p = _layer(
        x_prompt, zero_state, mem_k_p, mem_v_p,
        lambda aq, ak, av, av_t: _moba_prompt(aq, ak, av_t, bias_tiles), w)

    bs, ts, _ = x_sample.shape
    rb_lanes = jnp.repeat(rel_bias, ts, axis=1)
    cached_b, new_b = _sample_bias(rb_lanes, cache_k.shape[1], ts)
    y_s, k_s, v_s, c_s, n_s, m_s = _layer(
        x_sample, (state_C, state_n, state_m),
        cache_mem_k, cache_mem_v,
        lambda aq, ak, av, av_t: _moba_sample(aq, ak, av, cache_k, cache_v, page_table, cached_b, new_b, bs), w)

    return (y_p, y_s, k_p, v_p, mem_k_p.reshape(bp, n_mem, X_HEADS, X_DH),
            mem_v_p.reshape(bp, n_mem, X_HEADS, X_DH), c_p, n_p, m_p, k_s, v_s, c_s, n_s, m_s)
```

```python
import functools
import math

import jax
import jax.numpy as jnp
from jax import lax
from jax.experimental import pallas as pl
from jax.experimental.pallas import tpu as pltpu

F32 = jnp.float32
BF16 = jnp.bfloat16

EPS = 1e-6
D_MODEL = 1024
ML_HEADS, ML_DQK, ML_DV = 4, 128, 256
GATE_CAP = 15.0
MB_HEADS, MB_DH, MB_BLOCK, MB_TOPK = 8, 128, 256, 3
X_HEADS, X_DH = 4, 256
REL_BUCKETS, REL_MAX_DIST = 32, 1024
NEG = -1e30
BIG = 1e30
LOG2E = math.log2(math.e)
NEAR_BLOCKS = 5
N_SEG = 10

VMEM_LIMIT = 56 * 1024 * 1024


def _cparams(*sem):
    return pltpu.CompilerParams(dimension_semantics=sem, vmem_limit_bytes=VMEM_LIMIT)


def _resident(shape):
    nd = len(shape)
    return pl.BlockSpec(shape, lambda *_: (0,) * nd, pipeline_mode=pl.Buffered(1))


def _rms(x, g):
    return x * lax.rsqrt(jnp.mean(x * x, axis=-1, keepdims=True) + EPS) * g


def _head_rms(x, g, width):
    parts = [_rms(x[:, s:s + width], g) for s in range(0, x.shape[1], width)]
    return jnp.concatenate(parts, axis=1)


def _dot(a, b):
    return jnp.dot(a, b, preferred_element_type=F32)


def _dot_nt(a, b):
    return lax.dot_general(a, b, (((1,), (1,)), ((), ())), preferred_element_type=F32)


def _dot_tn(a, b):
    return lax.dot_general(a, b, (((0,), (0,)), ((), ())), preferred_element_type=F32)


def _split_hi_lo(x):
    hi = x.astype(BF16)
    lo = (x - hi.astype(F32)).astype(BF16)
    return hi, lo


def _rel_bucket(dist):
    max_exact = REL_BUCKETS // 2
    d = jnp.maximum(dist, 1).astype(F32)
    large = max_exact + (jnp.log(d / max_exact) / math.log(REL_MAX_DIST / max_exact)
                         * (REL_BUCKETS - max_exact)).astype(jnp.int32)
    large = jnp.minimum(large, REL_BUCKETS - 1)
    return jnp.where(dist < max_exact, dist, large)


def _inproj_kernel(x_ref, g1_ref, w_ref, wif_ref, bif_ref, qn_ref, kn_ref, cqn_ref,
                   qk_ref, v_ref, o_ref, aq_ref, ak32_ref, ak16_ref, av32_ref, av16_ref, avt_ref,
                   cq_ref, g_ref, gif_ref):
    xn = _rms(x_ref[...], g1_ref[...])
    xb = xn.astype(BF16)

    def seg(s):
        return _dot(xb, w_ref[:, s * 1024:(s + 1) * 1024])

    lane = lax.broadcasted_iota(jnp.int32, (1, 1024), 1)
    kscale = jnp.where(lane >= ML_HEADS * ML_DQK, ML_DQK ** -0.5, 1.0).astype(F32)
    qk_ref[...] = (seg(0) * kscale).astype(BF16)
    v_ref[...] = seg(1).astype(BF16)
    o_ref[...] = jax.nn.sigmoid(seg(2))
    aq_ref[...] = _head_rms(seg(3), qn_ref[...], MB_DH).astype(BF16)
    ak = _head_rms(seg(4), kn_ref[...], MB_DH)
    ak32_ref[...] = ak
    ak16_ref[...] = ak.astype(BF16)
    av = seg(5)
    av32_ref[...] = av
    av16_ref[...] = av.astype(BF16)
    avt_ref[0] = av.T.astype(BF16)
    cq_ref[...] = _head_rms(seg(6), cqn_ref[...], X_DH).astype(BF16)
    for s in range(3):
        g_ref[:, s * 1024:(s + 1) * 1024] = jax.nn.sigmoid(seg(7 + s))

    x_lo = (xn - xb.astype(F32)).astype(BF16)
    w_hi, w_lo = _split_hi_lo(wif_ref[...])
    z = _dot(xb, w_hi) + (_dot(x_lo, w_hi) + _dot(xb, w_lo)) + bif_ref[...]
    capped = GATE_CAP * jnp.tanh(z / GATE_CAP)
    log_sig = -(jnp.maximum(-capped, 0.0) + jnp.log1p(jnp.exp(-jnp.abs(capped))))
    col = lax.broadcasted_iota(jnp.int32, z.shape, 1)
    gif_ref[...] = jnp.where(col < ML_HEADS, capped, log_sig)


def _inproj(x, norm1_g, w_main, w_if, b_if, qn_g, kn_g, cqn_g, tm=256):
    n = x.shape[0]
    tm = min(tm, n)
    assert n % tm == 0
    row = lambda w: pl.BlockSpec((tm, w), lambda i: (i, 0))
    shapes = [(1024, BF16), (1024, BF16), (1024, F32), (1024, BF16), (1024, F32), (1024, BF16),
              (1024, F32), (1024, BF16), None, (1024, BF16), (3072, F32), (2 * ML_HEADS, F32)]
    out_specs = [pl.BlockSpec((1, 1024, tm), lambda i: (i, 0, 0)) if s is None else row(s[0]) for s in shapes]
    out_shape = [jax.ShapeDtypeStruct((n // tm, 1024, tm), BF16) if s is None
                 else jax.ShapeDtypeStruct((n, s[0]), s[1]) for s in shapes]
    return pl.pallas_call(
        _inproj_kernel,
        grid=(n // tm,),
        in_specs=[row(1024), _resident((1, 1024)), _resident(w_main.shape), _resident(w_if.shape),
                  _resident((1, 2 * ML_HEADS)), _resident((1, MB_DH)), _resident((1, MB_DH)),
                  _resident((1, X_DH))],
        out_specs=out_specs,
        out_shape=out_shape,
        compiler_params=_cparams("parallel"),
        name="inproj",
    )(x, norm1_g, w_main, w_if, b_if, qn_g, kn_g, cqn_g)


def _mlstm_kernel(qk_ref, v_ref, o_ref, gc_ref, gr_ref, c0_ref, n0_ref, m0_ref, hn_ref,
                  y_ref, cout_ref, nout_ref, mout_ref, c_s, n_s, m_s, *, chunk, n_chunks):
    L = chunk
    c = pl.program_id(1)

    @pl.when(c == 0)
    def _():
        c_s[...] = c0_ref[0]
        n_s[...] = n0_ref[0]
        m_s[...] = m0_ref[0]

    gc = gc_ref[0]
    gr = gr_ref[0]
    t_idx = lax.broadcasted_iota(jnp.int32, (L, L), 0)
    s_idx = lax.broadcasted_iota(jnp.int32, (L, L), 1)
    causal = s_idx <= t_idx

    for h in range(ML_HEADS):
        ig_row, lf_row = gr[h:h + 1, :], gr[ML_HEADS + h:ML_HEADS + h + 1, :]
        ig_col, lf_col = gc[:, h:h + 1], gc[:, ML_HEADS + h:ML_HEADS + h + 1]
        b_col = jnp.sum(jnp.where(causal, lf_row, 0.0), axis=1, keepdims=True)
        b_row = jnp.sum(jnp.where(t_idx <= s_idx, lf_col, 0.0), axis=0, keepdims=True)
        m_prev = m_s[:, h:h + 1]

        log_d = jnp.where(causal, b_col - b_row + ig_row, -jnp.inf)
        m_inter = b_col + m_prev
        m_t = jnp.maximum(m_inter, jnp.max(log_d, axis=1, keepdims=True))
        w_intra = jnp.exp(log_d - m_t)
        w_inter = jnp.exp(m_inter - m_t)

        q = qk_ref[0, :, h * ML_DQK:(h + 1) * ML_DQK]
        k = qk_ref[0, :, (ML_HEADS + h) * ML_DQK:(ML_HEADS + h + 1) * ML_DQK]
        v = v_ref[0, :, h * ML_DV:(h + 1) * ML_DV]
        c_h = c_s[h]
        n_h = n_s[h:h + 1, :]

        s = _dot_nt(q, k) * w_intra
        num = w_inter * _dot(q, c_h.astype(BF16)) + _dot(s.astype(BF16), v)
        qn = jnp.sum(q.astype(F32) * n_h, axis=1, keepdims=True)
        den = w_inter * qn + jnp.sum(s, axis=1, keepdims=True)
        hh = num / jnp.maximum(jnp.abs(den), jnp.exp(-m_t))

        b_last = b_col[L - 1:L, :]
        m_new = jnp.maximum(b_last + m_prev,
                            jnp.max(b_last - b_row + ig_row, axis=1, keepdims=True))
        w_col = jnp.exp(b_last - b_col + ig_col - m_new)
        decay = jnp.exp(b_last + m_prev - m_new)
        kw = k.astype(F32) * w_col
        c_s[h] = decay * c_h + _dot_tn(kw.astype(BF16), v)
        n_s[h:h + 1, :] = decay * n_h + jnp.sum(kw, axis=0, keepdims=True)
        m_s[:, h:h + 1] = m_new

        hn = _rms(hh, hn_ref[:, h * ML_DV:(h + 1) * ML_DV])
        y_ref[0, :, h * ML_DV:(h + 1) * ML_DV] = (o_ref[0, :, h * ML_DV:(h + 1) * ML_DV] * hn).astype(BF16)

    @pl.when(c == n_chunks - 1)
    def _():
        cout_ref[0] = c_s[...]
        nout_ref[0] = n_s[...]
        mout_ref[0] = m_s[...]


def _mlstm(qk, v, o_sig, gif, c0, n0, m0, hn_g, batch, chunk):
    n = qk.shape[0]
    t = n // batch
    nc = t // chunk
    chunked = lambda a: a.reshape(batch * nc, chunk, a.shape[-1])
    gc = chunked(gif)
    gr = gc.transpose(0, 2, 1)
    row = lambda w: pl.BlockSpec((1, chunk, w), lambda b, c: (b * nc + c, 0, 0))
    state = lambda shape: pl.BlockSpec((1,) + shape, lambda b, c: (b,) + (0,) * len(shape))
    y, c_out, n_out, m_out = pl.pallas_call(
        functools.partial(_mlstm_kernel, chunk=chunk, n_chunks=nc),
        grid=(batch, nc),
        in_specs=[row(1024), row(1024), row(1024), row(2 * ML_HEADS),
                  pl.BlockSpec((1, 2 * ML_HEADS, chunk), lambda b, c: (b * nc + c, 0, 0)),
                  state((ML_HEADS, ML_DQK, ML_DV)), state((ML_HEADS, ML_DQK)), state((1, ML_HEADS)),
                  _resident((1, ML_HEADS * ML_DV))],
        out_specs=[row(1024), state((ML_HEADS, ML_DQK, ML_DV)), state((ML_HEADS, ML_DQK)),
                   state((1, ML_HEADS))],
        out_shape=[jax.ShapeDtypeStruct((batch * nc, chunk, 1024), BF16),
                   jax.ShapeDtypeStruct((batch, ML_HEADS, ML_DQK, ML_DV), F32),
                   jax.ShapeDtypeStruct((batch, ML_HEADS, ML_DQK), F32),
                   jax.ShapeDtypeStruct((batch, 1, ML_HEADS), F32)],
        scratch_shapes=[pltpu.VMEM((ML_HEADS, ML_DQK, ML_DV), F32), pltpu.VMEM((ML_HEADS, ML_DQK), F32),
                        pltpu.VMEM((1, ML_HEADS), F32)],
        compiler_params=_cparams("parallel", "arbitrary"),
        name="mlstm",
    )(chunked(qk), chunked(v), chunked(o_sig), gc, gr, c0, n0, m0.reshape(batch, 1, ML_HEADS), hn_g)
    return y.reshape(n, 1024), c_out, n_out, m_out


def _bucket_lookup(bucket, table_row):
    val = jnp.zeros(bucket.shape, F32)
    for b in range(REL_BUCKETS):
        val = jnp.where(bucket == b, table_row(b), val)
    return val


def _prompt_bias_kernel(rb_ref, o_ref):
    h = pl.program_id(0)
    kj = lax.broadcasted_iota(jnp.int32, (MB_BLOCK, MB_BLOCK), 0)
    qi = lax.broadcasted_iota(jnp.int32, (MB_BLOCK, MB_BLOCK), 1)
    for d in range(NEAR_BLOCKS + 1):
        bucket = _rel_bucket(jnp.maximum(d * MB_BLOCK + qi - kj, 0))
        bias = _bucket_lookup(bucket, lambda b: rb_ref[b, h]) * LOG2E
        o_ref[0, d] = jnp.where(kj <= qi, bias, NEG) if d == 0 else bias


def _prompt_bias(rel_bias):
    return pl.pallas_call(
        _prompt_bias_kernel,
        grid=(MB_HEADS,),
        in_specs=[pl.BlockSpec(memory_space=pltpu.SMEM)],
        out_specs=pl.BlockSpec((1, NEAR_BLOCKS + 1, MB_BLOCK, MB_BLOCK), lambda h: (h, 0, 0, 0)),
        out_shape=jax.ShapeDtypeStruct((MB_HEADS, NEAR_BLOCKS + 1, MB_BLOCK, MB_BLOCK), F32),
        compiler_params=_cparams("parallel"),
        name="prompt_bias",
    )(rel_bias)


N_NEAR = (NEAR_BLOCKS - 1) * MB_BLOCK


def _sample_bias_kernel(rbl_ref, cached_ref, new_ref, *, page, t_new):
    lanes = cached_ref.shape[1]
    lookup = lambda bucket: _bucket_lookup(bucket, lambda b: rbl_ref[b:b + 1, :])
    r = lax.broadcasted_iota(jnp.int32, cached_ref.shape, 0)
    tq = lax.broadcasted_iota(jnp.int32, cached_ref.shape, 1) % t_new
    cached_ref[...] = lookup(_rel_bucket(jnp.where(r < page, N_NEAR + 1, tq + N_NEAR + page - r)))
    r = lax.broadcasted_iota(jnp.int32, (t_new, lanes), 0)
    tq = lax.broadcasted_iota(jnp.int32, (t_new, lanes), 1) % t_new
    new_ref[...] = lookup(_rel_bucket(jnp.maximum(tq - r, 0)))


def _sample_bias(rb_lanes, page, t_new):
    lanes = rb_lanes.shape[1]
    return pl.pallas_call(
        functools.partial(_sample_bias_kernel, page=page, t_new=t_new),
        out_shape=[jax.ShapeDtypeStruct((page + N_NEAR, lanes), F32), jax.ShapeDtypeStruct((t_new, lanes), F32)],
        name="sample_bias",
    )(rb_lanes)


MOBA_GROUP = 4
MOBA_RUNS = (8, 4, 2, 1)
ONES_ROWS = 16


def _run_lengths(n_groups, fn, carry):
    done = 0
    for idx, run in enumerate(MOBA_RUNS):
        trips = (n_groups - done) // run if idx == 0 else ((n_groups - done) // run) % 2
        first = done

        def body(r, c, first=first, run=run):
            return fn((first + r * run) * MOBA_GROUP, run * MOBA_GROUP, c)
        carry = lax.fori_loop(0, trips, body, carry)
        done = done + trips * run
    return carry


def _moba_prompt_kernel(q_ref, k_ref, vt_ref, bias_ref, o_ref, means_s, t_s, *, n_blocks):
    i = pl.program_id(1)
    kblk = lambda j: k_ref[pl.ds(pl.multiple_of(j * MB_BLOCK, MB_BLOCK), MB_BLOCK), :]

    @pl.when(i == 0)
    def _():
        def body(j, carry):
            means_s[pl.ds(j, 1), :] = jnp.sum(kblk(j).astype(F32), axis=0, keepdims=True) * (1.0 / MB_BLOCK)
            return carry
        lax.fori_loop(0, n_blocks, body, 0)

    q = q_ref[...]
    c = MB_DH ** -0.5 * LOG2E
    m_hi, m_lo = _split_hi_lo(means_s[...])
    gate = _dot_nt(m_hi, q) + _dot_nt(m_lo, q)
    blk = lax.broadcasted_iota(jnp.int32, gate.shape, 0)
    gate = jnp.where(blk < i, gate, NEG)
    sel = []
    for _ in range(MB_TOPK):
        mx = jnp.max(gate, axis=0, keepdims=True)
        idx = jnp.min(jnp.where(gate == mx, blk, n_blocks), axis=0, keepdims=True)
        sel.append(jnp.where(idx < i, idx, -1))
        gate = jnp.where(blk == idx, -jnp.inf, gate)

    n_groups = (i + MOBA_GROUP) // MOBA_GROUP
    clamp = lambda jt: jnp.minimum(jt, n_blocks - 1)

    def chosen(jt):
        return (sel[0] == jt) | (sel[1] == jt) | (sel[2] == jt) | (jt == i)

    def logits_and_max(first, count, m):
        for u in range(count):
            jt = first + u
            t = _dot_nt(kblk(clamp(jt)), q) * c + bias_ref[0, jnp.clip(i - jt, 0, NEAR_BLOCKS)]
            t_s[jt] = t
            m = jnp.maximum(m, jnp.where(chosen(jt), jnp.max(t, axis=0, keepdims=True), NEG))
        return m

    m = _run_lengths(n_groups, logits_and_max, jnp.full((1, MB_BLOCK), NEG, F32))

    ones = jnp.ones((ONES_ROWS, MB_BLOCK), BF16)

    def accumulate(first, count, acc):
        for u in range(count):
            jt = first + u
            shift = jnp.where(chosen(jt), m, BIG)
            p = jnp.exp2((t_s[jt] - shift).astype(BF16))
            acc = acc + _dot(jnp.concatenate([vt_ref[clamp(jt)], ones], axis=0), p)
        return acc

    acc = _run_lengths(n_groups, accumulate, jnp.zeros((MB_DH + ONES_ROWS, MB_BLOCK), F32))
    o_ref[...] = (acc[:MB_DH] / acc[MB_DH:MB_DH + 1]).T.astype(BF16)


def _moba_prompt(aq, ak, av_t, bias_tiles):
    n = aq.shape[0]
    assert n % MB_BLOCK == 0
    nb = n // MB_BLOCK
    assert av_t.shape == (nb, MB_HEADS * MB_DH, MB_BLOCK)
    n_tiles = -(-nb // MOBA_GROUP) * MOBA_GROUP
    return pl.pallas_call(
        functools.partial(_moba_prompt_kernel, n_blocks=nb),
        grid=(MB_HEADS, nb),
        in_specs=[pl.BlockSpec((MB_BLOCK, MB_DH), lambda h, i: (i, h)),
                  pl.BlockSpec((n, MB_DH), lambda h, i: (0, h)),
                  pl.BlockSpec((nb, MB_DH, MB_BLOCK), lambda h, i: (0, h, 0)),
                  pl.BlockSpec((1, NEAR_BLOCKS + 1, MB_BLOCK, MB_BLOCK), lambda h, i: (h, 0, 0, 0))],
        out_specs=pl.BlockSpec((MB_BLOCK, MB_DH), lambda h, i: (i, h)),
        out_shape=jax.ShapeDtypeStruct((n, MB_HEADS * MB_DH), BF16),
        scratch_shapes=[pltpu.VMEM((nb, MB_DH), F32), pltpu.VMEM((n_tiles, MB_BLOCK, MB_BLOCK), F32)],
        compiler_params=_cparams("parallel", "arbitrary"),
        name="moba_prompt",
    )(aq, ak, av_t, bias_tiles)


PAGES_PER_STEP = 8
PAGE_RING = 3


def _page_ring_step(pt_ref, cache_hbm, buf, sem, *, batch, n_steps):
    lin = pl.program_id(0) * n_steps + pl.program_id(1)
    n_lin = batch * n_steps

    def copies(l):
        b, s, slot = l // n_steps, l % n_steps, l % PAGE_RING
        return [pltpu.make_async_copy(cache_hbm.at[pt_ref[b, s * PAGES_PER_STEP + r]],
                                      buf.at[slot, r], sem.at[slot, r]) for r in range(PAGES_PER_STEP)]

    def start(l):
        for r, cp in enumerate(copies(l)):
            cp.start(priority=r % 2)

    @pl.when(lin == 0)
    def _():
        for l in range(min(PAGE_RING - 1, n_lin)):
            start(jnp.int32(l))

    @pl.when(lin + PAGE_RING - 1 < n_lin)
    def _():
        start(lin + PAGE_RING - 1)

    for cp in copies(lin):
        cp.wait()
    return lin % PAGE_RING


def _page_heads_on_lanes(buf, slot, r, page):
    heads = [buf[slot, r, pl.ds(h, page, stride=MB_HEADS), :] for h in range(MB_HEADS)]
    return jnp.concatenate(heads, axis=1).astype(BF16)


def _moba_sample_scores_kernel(pt_ref, qbd_ref, k_hbm, knew_ref, bias_ref, newb_ref, p_ref, pnew_ref, l_ref,
                               t_s, gate_s, bmax_s, kbuf, ksem, *, page, past, t_new, batch):
    step = pl.program_id(1)
    n_steps = pl.num_programs(1)
    slot = _page_ring_step(pt_ref, k_hbm, kbuf, ksem, batch=batch, n_steps=past // page // PAGES_PER_STEP)
    nbp = past // MB_BLOCK
    ppb = MB_BLOCK // page
    n_far_pages = (past - N_NEAR) // page
    lanes = MB_HEADS * t_new
    scale = MB_DH ** -0.5
    qbd = qbd_ref[0]

    for b in range(PAGES_PER_STEP // ppb):
        gate, bmax = 0.0, NEG
        for r in range(b * ppb, (b + 1) * ppb):
            pg = step * PAGES_PER_STEP + r
            s = _dot(_page_heads_on_lanes(kbuf, slot, r, page), qbd)
            brow = pl.multiple_of(jnp.maximum(pg - n_far_pages + 1, 0) * page, page)
            t = s * scale + bias_ref[pl.ds(brow, page), :]
            t_s[pl.ds(pl.multiple_of(pg * page, page), page), :] = t
            gate = gate + jnp.sum(s, axis=0, keepdims=True)
            bmax = jnp.maximum(bmax, jnp.max(t, axis=0, keepdims=True))
        blk = step * (PAGES_PER_STEP // ppb) + b
        gate_s[pl.ds(blk, 1), :] = gate * (1.0 / MB_BLOCK)
        bmax_s[pl.ds(blk, 1), :] = bmax

    @pl.when(step == n_steps - 1)
    def _():
        gate = gate_s[...]
        bidx = lax.broadcasted_iota(jnp.int32, gate.shape, 0)
        chosen = jnp.zeros(gate.shape, jnp.bool_)
        for _ in range(MB_TOPK):
            mx = jnp.max(gate, axis=0, keepdims=True)
            idx = jnp.min(jnp.where(gate == mx, bidx, nbp), axis=0, keepdims=True)
            chosen = chosen | ((bidx == idx) & (mx > -jnp.inf))
            gate = jnp.where(bidx == idx, -jnp.inf, gate)

        tk = lax.broadcasted_iota(jnp.int32, (t_new, lanes), 0)
        tq = lax.broadcasted_iota(jnp.int32, (t_new, lanes), 1) % t_new
        t_new_keys = jnp.where(tk <= tq, _dot(knew_ref[0], qbd) * scale + newb_ref[...], NEG)

        m = jnp.maximum(jnp.max(t_new_keys, axis=0, keepdims=True),
                        jnp.max(jnp.where(chosen, bmax_s[...], NEG), axis=0, keepdims=True))
        e_new = jnp.exp(t_new_keys - m)
        pnew_ref[0] = e_new
        gate_s[...] = jnp.where(chosen, 0.0, NEG)

        def exp_block(j, l):
            rows = pl.ds(pl.multiple_of(j * MB_BLOCK, MB_BLOCK), MB_BLOCK)
            e = jnp.exp(jnp.minimum(t_s[rows, :] - m, 0.0) + gate_s[pl.ds(j, 1), :])
            p_ref[0, rows, :] = e.astype(BF16)
            return l + jnp.sum(e, axis=0, keepdims=True)

        l_ref[0] = lax.fori_loop(0, nbp, exp_block, jnp.sum(e_new, axis=0, keepdims=True))


def _moba_sample_values_kernel(pt_ref, p_ref, v_hbm, pnew_ref, vnew_ref, l_ref, o_ref, acc_s, vbuf, vsem,
                               *, page, t_new, batch, n_steps):
    step = pl.program_id(1)
    slot = _page_ring_step(pt_ref, v_hbm, vbuf, vsem, batch=batch, n_steps=n_steps)

    @pl.when(step == 0)
    def _():
        acc_s[...] = jnp.zeros_like(acc_s)

    for r in range(PAGES_PER_STEP):
        acc_s[...] += _dot_tn(p_ref[0, r * page:(r + 1) * page, :], _page_heads_on_lanes(vbuf, slot, r, page))

    @pl.when(step == pl.num_programs(1) - 1)
    def _():
        acc = acc_s[...] + _dot_tn(pnew_ref[0].astype(BF16), vnew_ref[0])
        acc = acc / l_ref[0]
        parts = [acc[h * t_new:(h + 1) * t_new, h * MB_DH:(h + 1) * MB_DH] for h in range(MB_HEADS)]
        o_ref[0] = jnp.concatenate(parts, axis=1).astype(BF16)


def _moba_sample(aq, ak16, av16, cache_k, cache_v, page_table, cached_b, new_b, batch):
    t_new = aq.shape[0] // batch
    n_pool, page = cache_k.shape[:2]
    n_pages = page_table.shape[1]
    past = n_pages * page
    lanes = MB_HEADS * t_new
    assert MB_BLOCK % page == 0 and PAGES_PER_STEP % (MB_BLOCK // page) == 0
    assert n_pages % PAGES_PER_STEP == 0 and past % MB_BLOCK == 0 and past >= N_NEAR
    n_steps = n_pages // PAGES_PER_STEP
    ck = cache_k.reshape(n_pool, page * MB_HEADS, MB_DH)
    cv = cache_v.reshape(n_pool, page * MB_HEADS, MB_DH)
    q4 = aq.reshape(batch, t_new, MB_HEADS, MB_DH)
    qbd = jnp.einsum('bthd,hg->bhdgt', q4, jnp.eye(MB_HEADS, dtype=aq.dtype)).reshape(
        batch, MB_HEADS * MB_DH, lanes)

    per_batch = lambda shape: pl.BlockSpec((1,) + shape, lambda b, s, pt: (b,) + (0,) * len(shape))
    const = lambda shape: pl.BlockSpec(shape, lambda b, s, pt: (0,) * len(shape))
    in_hbm = pl.BlockSpec(memory_space=pl.ANY)
    ring_scratch = [pltpu.VMEM((PAGE_RING, PAGES_PER_STEP, page * MB_HEADS, MB_DH), F32),
                    pltpu.SemaphoreType.DMA((PAGE_RING, PAGES_PER_STEP))]

    p, p_new, l = pl.pallas_call(
        functools.partial(_moba_sample_scores_kernel, page=page, past=past, t_new=t_new, batch=batch),
        grid_spec=pltpu.PrefetchScalarGridSpec(
            num_scalar_prefetch=1,
            grid=(batch, n_steps),
            in_specs=[per_batch((MB_HEADS * MB_DH, lanes)), in_hbm, per_batch((t_new, MB_HEADS * MB_DH)),
                      const(cached_b.shape), const(new_b.shape)],
            out_specs=[per_batch((past, lanes)), per_batch((t_new, lanes)), per_batch((1, lanes))],
            scratch_shapes=[pltpu.VMEM((past, lanes), F32), pltpu.VMEM((past // MB_BLOCK, lanes), F32),
                            pltpu.VMEM((past // MB_BLOCK, lanes), F32)] + ring_scratch),
        out_shape=[jax.ShapeDtypeStruct((batch, past, lanes), BF16),
                   jax.ShapeDtypeStruct((batch, t_new, lanes), F32),
                   jax.ShapeDtypeStruct((batch, 1, lanes), F32)],
        compiler_params=_cparams("arbitrary", "arbitrary"),
        name="moba_sample_scores",
    )(page_table, qbd, ck, ak16.reshape(batch, t_new, -1), cached_b, new_b)

    y = pl.pallas_call(
        functools.partial(_moba_sample_values_kernel, page=page, t_new=t_new, batch=batch, n_steps=n_steps),
        grid_spec=pltpu.PrefetchScalarGridSpec(
            num_scalar_prefetch=1,
            grid=(batch, n_steps),
            in_specs=[pl.BlockSpec((1, PAGES_PER_STEP * page, lanes), lambda b, s, pt: (b, s, 0)), in_hbm,
                      per_batch((t_new, lanes)), per_batch((t_new, MB_HEADS * MB_DH)), per_batch((lanes, 1))],
            out_specs=per_batch((t_new, MB_HEADS * MB_DH)),
            scratch_shapes=[pltpu.VMEM((lanes, MB_HEADS * MB_DH), F32)] + ring_scratch),
        out_shape=jax.ShapeDtypeStruct((batch, t_new, MB_HEADS * MB_DH), BF16),
        compiler_params=_cparams("arbitrary", "arbitrary"),
        name="moba_sample_values",
    )(page_table, p, cv, p_new, av16.reshape(batch, t_new, -1), l.reshape(batch, lanes, 1))
    return y.reshape(batch * t_new, MB_HEADS * MB_DH)


def _memkv_kernel(mem_ref, g_ref, w_ref, kn_ref, mk_ref, mv_ref):
    xb = _rms(mem_ref[...], g_ref[...]).astype(BF16)
    width = X_HEADS * X_DH
    mk_ref[...] = _head_rms(_dot(xb, w_ref[:, :width]), kn_ref[...], X_DH)
    mv_ref[...] = _dot(xb, w_ref[:, width:])


def _memkv(mem, mem_norm_g, w_mem_kv, cross_kn_g):
    m = mem.shape[0]
    width = X_HEADS * X_DH
    return pl.pallas_call(
        _memkv_kernel,
        out_shape=[jax.ShapeDtypeStruct((m, width), F32), jax.ShapeDtypeStruct((m, width), F32)],
        compiler_params=pltpu.CompilerParams(vmem_limit_bytes=VMEM_LIMIT),
        name="memkv",
    )(mem, mem_norm_g, w_mem_kv, cross_kn_g)


def _cross_kernel(q_ref, mk_ref, mv_ref, o_ref, *, head_major_rows):
    scale = X_DH ** -0.5
    lane_tiles = X_DH // 128

    def head(ref, h):
        if not head_major_rows:
            return ref[0, :, h * X_DH:(h + 1) * X_DH].astype(BF16)
        n_mem = ref.shape[1] // (X_HEADS * lane_tiles)
        parts = [ref[0, pl.ds(h * lane_tiles + c, n_mem, stride=X_HEADS * lane_tiles), :]
                 for c in range(lane_tiles)]
        return jnp.concatenate(parts, axis=1).astype(BF16)

    for h in range(X_HEADS):
        cols = slice(h * X_DH, (h + 1) * X_DH)
        s = _dot_nt(q_ref[0, :, cols], head(mk_ref, h)) * scale
        p = jnp.exp(s - jnp.max(s, axis=1, keepdims=True))
        p = p / jnp.sum(p, axis=1, keepdims=True)
        o_ref[0, :, cols] = _dot(p.astype(BF16), head(mv_ref, h)).astype(BF16)


def _cross(cq, mk, mv, batch, tq):
    n = cq.shape[0]
    t = n // batch
    assert t % tq == 0
    nt = t // tq
    width = X_HEADS * X_DH
    head_major_rows = mk.ndim == 4
    if head_major_rows:
        mk, mv = (a.reshape(batch, -1, 128) for a in (mk, mv))
    q_spec = pl.BlockSpec((1, tq, width), lambda b, i: (b * nt + i, 0, 0))
    mem_spec = pl.BlockSpec((1,) + mk.shape[1:], lambda b, i: (b, 0, 0))
    y = pl.pallas_call(
        functools.partial(_cross_kernel, head_major_rows=head_major_rows),
        grid=(batch, nt),
        in_specs=[q_spec, mem_spec, mem_spec],
        out_specs=q_spec,
        out_shape=jax.ShapeDtypeStruct((batch * nt, tq, width), BF16),
        compiler_params=_cparams("parallel", "arbitrary"),
        name="cross",
    )(cq.reshape(batch * nt, tq, width), mk, mv)
    return y.reshape(n, width)


def _merge_kernel(x_ref, ym_ref, ya_ref, yc_ref, g_ref, wm_ref, wa_ref, wc_ref, wo_ref, o_ref):
    mixed = (g_ref[:, 0:1024] * _dot(ym_ref[...], wm_ref[...])
             + g_ref[:, 1024:2048] * _dot(ya_ref[...], wa_ref[...])
             + g_ref[:, 2048:3072] * _dot(yc_ref[...], wc_ref[...]))
    o_ref[...] = x_ref[...] + _dot(mixed.astype(BF16), wo_ref[...])


def _merge(x, y_m, y_a, y_c, g, w_m, w_a, w_c, w_o, tm=512):
    n = x.shape[0]
    tm = min(tm, n)
    assert n % tm == 0
    row = lambda w: pl.BlockSpec((tm, w), lambda i: (i, 0))
    return pl.pallas_call(
        _merge_kernel,
        grid=(n // tm,),
        in_specs=[row(1024), row(1024), row(1024), row(1024), row(3072)]
                 + [_resident((1024, 1024))] * 4,
        out_specs=row(1024),
        out_shape=jax.ShapeDtypeStruct((n, 1024), F32),
        compiler_params=_cparams("parallel"),
        name="merge",
    )(x, y_m, y_a, y_c, g, w_m, w_a, w_c, w_o)


def _swiglu_kernel(x_ref, g_ref, wgu_ref, wd_ref, o_ref):
    x = x_ref[...]
    gu = _dot(_rms(x, g_ref[...]).astype(BF16), wgu_ref[...])
    d_ff = wd_ref.shape[0]
    gt, up = gu[:, :d_ff], gu[:, d_ff:]
    o_ref[...] = x + _dot((gt * jax.nn.sigmoid(gt) * up).astype(BF16), wd_ref[...])


def _swiglu(x, norm2_g, w_gu, w_down, tm=512):
    n = x.shape[0]
    tm = min(tm, n)
    assert n % tm == 0
    row = pl.BlockSpec((tm, 1024), lambda i: (i, 0))
    return pl.pallas_call(
        _swiglu_kernel,
        grid=(n // tm,),
        in_specs=[row, _resident((1, 1024)), _resident(w_gu.shape), _resident(w_down.shape)],
        out_specs=row,
        out_shape=jax.ShapeDtypeStruct((n, 1024), F32),
        compiler_params=_cparams("parallel"),
        name="swiglu",
    )(x, norm2_g, w_gu, w_down)


def _pick_chunk(t):
    for c in (256, 128, 64):
        if t % c == 0:
            return c
    return t


def _layer(x3, ml_state, mem_k, mem_v, moba, w):
    batch, t, _ = x3.shape
    x = x3.reshape(batch * t, D_MODEL)
    (qk, v, o_sig, aq, ak32, ak16, av32, av16, av_t, cq, g, gif) = _inproj(
        x, w['norm1_g'], w['w_main'], w['w_if'], w['b_if'], w['moba_qn_g'], w['moba_kn_g'], w['cross_qn_g'])
    y_m, c_out, n_out, m_out = _mlstm(qk, v, o_sig, gif, ml_state[0], ml_state[1], ml_state[2],
                                      w['mlstm_hn_g'], batch, _pick_chunk(t))
    y_a = moba(aq, ak16, av16, av_t)
    y_c = _cross(cq, mem_k, mem_v, batch, min(t, 1024))
    x1 = _merge(x, y_m, y_a, y_c, g, w['w_br_m'], w['w_br_a'], w['w_br_c'], w['w_out'])
    y = _swiglu(x1, w['norm2_g'], w['w_gu'], w['w_down'])
    return (y.reshape(batch, t, D_MODEL), ak32.reshape(batch, t, MB_HEADS, MB_DH),
            av32.reshape(batch, t, MB_HEADS, MB_DH), c_out, n_out, m_out.reshape(batch, ML_HEADS))


def kernel(x_prompt, x_sample, mem_prompt, cache_k, cache_v, cache_mem_k, cache_mem_v, state_C, state_n, state_m, page_table, norm1_g, w_in, b_ig, b_fg, mlstm_hn_g, moba_qn_g, moba_kn_g, rel_bias, cross_qn_g, cross_kn_g, mem_norm_g, w_mem_kv, w_br_m, w_br_a, w_br_c, w_out, norm2_g, w_gu, w_down):
    c_if = 2 * ML_HEADS * ML_DQK + 2 * ML_HEADS * ML_DV
    row = lambda a: a.reshape(1, -1)
    w = {
        'norm1_g': row(norm1_g),
        'w_main': jnp.concatenate([w_in[:, :c_if], w_in[:, c_if + 2 * ML_HEADS:]], axis=1).astype(BF16),
        'w_if': w_in[:, c_if:c_if + 2 * ML_HEADS],
        'b_if': row(jnp.concatenate([b_ig, b_fg])),
        'mlstm_hn_g': row(mlstm_hn_g), 'moba_qn_g': row(moba_qn_g), 'moba_kn_g': row(moba_kn_g),
        'cross_qn_g': row(cross_qn_g),
        'w_br_m': w_br_m.astype(BF16), 'w_br_a': w_br_a.astype(BF16), 'w_br_c': w_br_c.astype(BF16),
        'w_out': w_out.astype(BF16), 'norm2_g': row(norm2_g),
        'w_gu': w_gu.astype(BF16), 'w_down': w_down.astype(BF16),
    }
    assert w['w_main'].shape[1] == N_SEG * 1024

    bp, sp, _ = x_prompt.shape
    assert bp == 1
    zero_state = (jnp.zeros((bp, ML_HEADS, ML_DQK, ML_DV), F32), jnp.zeros((bp, ML_HEADS, ML_DQK), F32),
                  jnp.zeros((bp, ML_HEADS), F32))
    n_mem = mem_prompt.shape[1]
    mem_k_p, mem_v_p = _memkv(mem_prompt.reshape(bp * n_mem, D_MODEL), row(mem_norm_g),
                              w_mem_kv.astype(BF16), row(cross_kn_g))
    mem_k_p = mem_k_p.reshape(bp, n_mem, X_HEADS * X_DH)
    mem_v_p = mem_v_p.reshape(bp, n_mem, X_HEADS * X_DH)
    bias_tiles = _prompt_bias(rel_bias)
    y_p, k_p, v_p, c_p, n_p, m_p = _layer(
        x_prompt, zero_state, mem_k_p, mem_v_p,
        lambda aq, ak, av, av_t: _moba_prompt(aq, ak, av_t, bias_tiles), w)

    bs, ts, _ = x_sample.shape
    rb_lanes = jnp.repeat(rel_bias, ts, axis=1)
    cached_b, new_b = _sample_bias(rb_lanes, cache_k.shape[1], ts)
    y_s, k_s, v_s, c_s, n_s, m_s = _layer(
        x_sample, (state_C, state_n, state_m),
        cache_mem_k, cache_mem_v,
        lambda aq, ak, av, av_t: _moba_sample(aq, ak, av, cache_k, cache_v, page_table, cached_b, new_b, bs), w)

    return (y_p, y_s, k_p, v_p, mem_k_p.reshape(bp, n_mem, X_HEADS, X_DH),
            mem_v_p.reshape(bp, n_mem, X_HEADS, X_DH), c_p, n_p, m_p, k_s, v_s, c_s, n_s, m_s)
```

```python
import functools
import math

import jax
import jax.numpy as jnp
from jax import lax
from jax.experimental import pallas as pl
from jax.experimental.pallas import tpu as pltpu

F32 = jnp.float32
BF16 = jnp.bfloat16

EPS = 1e-6
D_MODEL = 1024
ML_HEADS, ML_DQK, ML_DV = 4, 128, 256
GATE_CAP = 15.0
MB_HEADS, MB_DH, MB_BLOCK, MB_TOPK = 8, 128, 256, 3
X_HEADS, X_DH = 4, 256
REL_BUCKETS, REL_MAX_DIST = 32, 1024
NEG = -1e30
BIG = 1e30
LOG2E = math.log2(math.e)
NEAR_BLOCKS = 5
N_SEG = 10

VMEM_LIMIT = 56 * 1024 * 1024


def _cparams(*sem):
    return pltpu.CompilerParams(dimension_semantics=sem, vmem_limit_bytes=VMEM_LIMIT)


def _resident(shape):
    nd = len(shape)
    return pl.BlockSpec(shape, lambda *_: (0,) * nd, pipeline_mode=pl.Buffered(1))


def _rms(x, g):
    return x * lax.rsqrt(jnp.mean(x * x, axis=-1, keepdims=True) + EPS) * g


def _head_rms(x, g, width):
    parts = [_rms(x[:, s:s + width], g) for s in range(0, x.shape[1], width)]
    return jnp.concatenate(parts, axis=1)


def _dot(a, b):
    return jnp.dot(a, b, preferred_element_type=F32)


def _dot_nt(a, b):
    return lax.dot_general(a, b, (((1,), (1,)), ((), ())), preferred_element_type=F32)


def _dot_tn(a, b):
    return lax.dot_general(a, b, (((0,), (0,)), ((), ())), preferred_element_type=F32)


def _split_hi_lo(x):
    hi = x.astype(BF16)
    lo = (x - hi.astype(F32)).astype(BF16)
    return hi, lo


def _rel_bucket(dist):
    max_exact = REL_BUCKETS // 2
    d = jnp.maximum(dist, 1).astype(F32)
    large = max_exact + (jnp.log(d / max_exact) / math.log(REL_MAX_DIST / max_exact)
                         * (REL_BUCKETS - max_exact)).astype(jnp.int32)
    large = jnp.minimum(large, REL_BUCKETS - 1)
    return jnp.where(dist < max_exact, dist, large)


def _inproj_kernel(x_ref, g1_ref, w_ref, wif_ref, bif_ref, qn_ref, kn_ref, cqn_ref,
                   qk_ref, v_ref, o_ref, aq_ref, ak32_ref, ak16_ref, av32_ref, av16_ref, avt_ref,
                   cq_ref, g_ref, gif_ref):
    xn = _rms(x_ref[...], g1_ref[...])
    xb = xn.astype(BF16)

    def seg(s):
        return _dot(xb, w_ref[:, s * 1024:(s + 1) * 1024])

    lane = lax.broadcasted_iota(jnp.int32, (1, 1024), 1)
    kscale = jnp.where(lane >= ML_HEADS * ML_DQK, ML_DQK ** -0.5, 1.0).astype(F32)
    qk_ref[...] = (seg(0) * kscale).astype(BF16)
    v_ref[...] = seg(1).astype(BF16)
    o_ref[...] = jax.nn.sigmoid(seg(2))
    aq_ref[...] = _head_rms(seg(3), qn_ref[...], MB_DH).astype(BF16)
    ak = _head_rms(seg(4), kn_ref[...], MB_DH)
    ak32_ref[...] = ak
    ak16_ref[...] = ak.astype(BF16)
    av = seg(5)
    av32_ref[...] = av
    av16_ref[...] = av.astype(BF16)
    avt_ref[0] = av.T.astype(BF16)
    cq_ref[...] = _head_rms(seg(6), cqn_ref[...], X_DH).astype(BF16)
    for s in range(3):
        g_ref[:, s * 1024:(s + 1) * 1024] = jax.nn.sigmoid(seg(7 + s))

    x_lo = (xn - xb.astype(F32)).astype(BF16)
    w_hi, w_lo = _split_hi_lo(wif_ref[...])
    z = _dot(xb, w_hi) + (_dot(x_lo, w_hi) + _dot(xb, w_lo)) + bif_ref[...]
    capped = GATE_CAP * jnp.tanh(z / GATE_CAP)
    log_sig = -(jnp.maximum(-capped, 0.0) + jnp.log1p(jnp.exp(-jnp.abs(capped))))
    col = lax.broadcasted_iota(jnp.int32, z.shape, 1)
    gif_ref[...] = jnp.where(col < ML_HEADS, capped, log_sig)


def _inproj(x, norm1_g, w_main, w_if, b_if, qn_g, kn_g, cqn_g, tm=256):
    n = x.shape[0]
    tm = min(tm, n)
    assert n % tm == 0
    row = lambda w: pl.BlockSpec((tm, w), lambda i: (i, 0))
    shapes = [(1024, BF16), (1024, BF16), (1024, F32), (1024, BF16), (1024, F32), (1024, BF16),
              (1024, F32), (1024, BF16), None, (1024, BF16), (3072, F32), (2 * ML_HEADS, F32)]
    out_specs = [pl.BlockSpec((1, 1024, tm), lambda i: (i, 0, 0)) if s is None else row(s[0]) for s in shapes]
    out_shape = [jax.ShapeDtypeStruct((n // tm, 1024, tm), BF16) if s is None
                 else jax.ShapeDtypeStruct((n, s[0]), s[1]) for s in shapes]
    return pl.pallas_call(
        _inproj_kernel,
        grid=(n // tm,),
        in_specs=[row(1024), _resident((1, 1024)), _resident(w_main.shape), _resident(w_if.shape),
                  _resident((1, 2 * ML_HEADS)), _resident((1, MB_DH)), _resident((1, MB_DH)),
                  _resident((1, X_DH))],
        out_specs=out_specs,
        out_shape=out_shape,
        compiler_params=_cparams("parallel"),
        name="inproj",
    )(x, norm1_g, w_main, w_if, b_if, qn_g, kn_g, cqn_g)


def _mlstm_kernel(qk_ref, v_ref, o_ref, gc_ref, gr_ref, c0_ref, n0_ref, m0_ref, hn_ref,
                  y_ref, cout_ref, nout_ref, mout_ref, c_s, n_s, m_s, *, chunk, n_chunks):
    L = chunk
    c = pl.program_id(1)

    @pl.when(c == 0)
    def _():
        c_s[...] = c0_ref[0]
        n_s[...] = n0_ref[0]
        m_s[...] = m0_ref[0]

    gc = gc_ref[0]
    gr = gr_ref[0]
    t_idx = lax.broadcasted_iota(jnp.int32, (L, L), 0)
    s_idx = lax.broadcasted_iota(jnp.int32, (L, L), 1)
    causal = s_idx <= t_idx

    for h in range(ML_HEADS):
        ig_row, lf_row = gr[h:h + 1, :], gr[ML_HEADS + h:ML_HEADS + h + 1, :]
        ig_col, lf_col = gc[:, h:h + 1], gc[:, ML_HEADS + h:ML_HEADS + h + 1]
        b_col = jnp.sum(jnp.where(causal, lf_row, 0.0), axis=1, keepdims=True)
        b_row = jnp.sum(jnp.where(t_idx <= s_idx, lf_col, 0.0), axis=0, keepdims=True)
        m_prev = m_s[:, h:h + 1]

        log_d = jnp.where(causal, b_col - b_row + ig_row, -jnp.inf)
        m_inter = b_col + m_prev
        m_t = jnp.maximum(m_inter, jnp.max(log_d, axis=1, keepdims=True))
        w_intra = jnp.exp(log_d - m_t)
        w_inter = jnp.exp(m_inter - m_t)

        q = qk_ref[0, :, h * ML_DQK:(h + 1) * ML_DQK]
        k = qk_ref[0, :, (ML_HEADS + h) * ML_DQK:(ML_HEADS + h + 1) * ML_DQK]
        v = v_ref[0, :, h * ML_DV:(h + 1) * ML_DV]
        c_h = c_s[h]
        n_h = n_s[h:h + 1, :]

        s = _dot_nt(q, k) * w_intra
        num = w_inter * _dot(q, c_h.astype(BF16)) + _dot(s.astype(BF16), v)
        qn = jnp.sum(q.astype(F32) * n_h, axis=1, keepdims=True)
        den = w_inter * qn + jnp.sum(s, axis=1, keepdims=True)
        hh = num / jnp.maximum(jnp.abs(den), jnp.exp(-m_t))

        b_last = b_col[L - 1:L, :]
        m_new = jnp.maximum(b_last + m_prev,
                            jnp.max(b_last - b_row + ig_row, axis=1, keepdims=True))
        w_col = jnp.exp(b_last - b_col + ig_col - m_new)
        decay = jnp.exp(b_last + m_prev - m_new)
        kw = k.astype(F32) * w_col
        c_s[h] = decay * c_h + _dot_tn(kw.astype(BF16), v)
        n_s[h:h + 1, :] = decay * n_h + jnp.sum(kw, axis=0, keepdims=True)
        m_s[:, h:h + 1] = m_new

        hn = _rms(hh, hn_ref[:, h * ML_DV:(h + 1) * ML_DV])
        y_ref[0, :, h * ML_DV:(h + 1) * ML_DV] = (o_ref[0, :, h * ML_DV:(h + 1) * ML_DV] * hn).astype(BF16)

    @pl.when(c == n_chunks - 1)
    def _():
        cout_ref[0] = c_s[...]
        nout_ref[0] = n_s[...]
        mout_ref[0] = m_s[...]


def _mlstm(qk, v, o_sig, gif, c0, n0, m0, hn_g, batch, chunk):
    n = qk.shape[0]
    t = n // batch
    nc = t // chunk
    chunked = lambda a: a.reshape(batch * nc, chunk, a.shape[-1])
    gc = chunked(gif)
    gr = gc.transpose(0, 2, 1)
    row = lambda w: pl.BlockSpec((1, chunk, w), lambda b, c: (b * nc + c, 0, 0))
    state = lambda shape: pl.BlockSpec((1,) + shape, lambda b, c: (b,) + (0,) * len(shape))
    y, c_out, n_out, m_out = pl.pallas_call(
        functools.partial(_mlstm_kernel, chunk=chunk, n_chunks=nc),
        grid=(batch, nc),
        in_specs=[row(1024), row(1024), row(1024), row(2 * ML_HEADS),
                  pl.BlockSpec((1, 2 * ML_HEADS, chunk), lambda b, c: (b * nc + c, 0, 0)),
                  state((ML_HEADS, ML_DQK, ML_DV)), state((ML_HEADS, ML_DQK)), state((1, ML_HEADS)),
                  _resident((1, ML_HEADS * ML_DV))],
        out_specs=[row(1024), state((ML_HEADS, ML_DQK, ML_DV)), state((ML_HEADS, ML_DQK)),
                   state((1, ML_HEADS))],
        out_shape=[jax.ShapeDtypeStruct((batch * nc, chunk, 1024), BF16),
                   jax.ShapeDtypeStruct((batch, ML_HEADS, ML_DQK, ML_DV), F32),
                   jax.ShapeDtypeStruct((batch, ML_HEADS, ML_DQK), F32),
                   jax.ShapeDtypeStruct((batch, 1, ML_HEADS), F32)],
        scratch_shapes=[pltpu.VMEM((ML_HEADS, ML_DQK, ML_DV), F32), pltpu.VMEM((ML_HEADS, ML_DQK), F32),
                        pltpu.VMEM((1, ML_HEADS), F32)],
        compiler_params=_cparams("parallel", "arbitrary"),
        name="mlstm",
    )(chunked(qk), chunked(v), chunked(o_sig), gc, gr, c0, n0, m0.reshape(batch, 1, ML_HEADS), hn_g)
    return y.reshape(n, 1024), c_out, n_out, m_out


def _bucket_lookup(bucket, table_row):
    val = jnp.zeros(bucket.shape, F32)
    for b in range(REL_BUCKETS):
        val = jnp.where(bucket == b, table_row(b), val)
    return val


def _prompt_bias_kernel(rb_ref, o_ref):
    h = pl.program_id(0)
    kj = lax.broadcasted_iota(jnp.int32, (MB_BLOCK, MB_BLOCK), 0)
    qi = lax.broadcasted_iota(jnp.int32, (MB_BLOCK, MB_BLOCK), 1)
    for d in range(NEAR_BLOCKS + 1):
        bucket = _rel_bucket(jnp.maximum(d * MB_BLOCK + qi - kj, 0))
        bias = _bucket_lookup(bucket, lambda b: rb_ref[b, h]) * LOG2E
        o_ref[0, d] = jnp.where(kj <= qi, bias, NEG) if d == 0 else bias


def _prompt_bias(rel_bias):
    return pl.pallas_call(
        _prompt_bias_kernel,
        grid=(MB_HEADS,),
        in_specs=[pl.BlockSpec(memory_space=pltpu.SMEM)],
        out_specs=pl.BlockSpec((1, NEAR_BLOCKS + 1, MB_BLOCK, MB_BLOCK), lambda h: (h, 0, 0, 0)),
        out_shape=jax.ShapeDtypeStruct((MB_HEADS, NEAR_BLOCKS + 1, MB_BLOCK, MB_BLOCK), F32),
        compiler_params=_cparams("parallel"),
        name="prompt_bias",
    )(rel_bias)


N_NEAR = (NEAR_BLOCKS - 1) * MB_BLOCK


def _sample_bias_kernel(rbl_ref, cached_ref, new_ref, *, page, t_new):
    lanes = cached_ref.shape[1]
    lookup = lambda bucket: _bucket_lookup(bucket, lambda b: rbl_ref[b:b + 1, :])
    r = lax.broadcasted_iota(jnp.int32, cached_ref.shape, 0)
    tq = lax.broadcasted_iota(jnp.int32, cached_ref.shape, 1) % t_new
    cached_ref[...] = lookup(_rel_bucket(jnp.where(r < page, N_NEAR + 1, tq + N_NEAR + page - r)))
    r = lax.broadcasted_iota(jnp.int32, (t_new, lanes), 0)
    tq = lax.broadcasted_iota(jnp.int32, (t_new, lanes), 1) % t_new
    new_ref[...] = lookup(_rel_bucket(jnp.maximum(tq - r, 0)))


def _sample_bias(rb_lanes, page, t_new):
    lanes = rb_lanes.shape[1]
    return pl.pallas_call(
        functools.partial(_sample_bias_kernel, page=page, t_new=t_new),
        out_shape=[jax.ShapeDtypeStruct((page + N_NEAR, lanes), F32), jax.ShapeDtypeStruct((t_new, lanes), F32)],
        name="sample_bias",
    )(rb_lanes)


MOBA_GROUP = 4
MOBA_RUNS = (8, 4, 2, 1)
ONES_ROWS = 16


def _run_lengths(n_groups, fn, carry):
    done = 0
    for idx, run in enumerate(MOBA_RUNS):
        trips = (n_groups - done) // run if idx == 0 else ((n_groups - done) // run) % 2
        first = done

        def body(r, c, first=first, run=run):
            return fn((first + r * run) * MOBA_GROUP, run * MOBA_GROUP, c)
        carry = lax.fori_loop(0, trips, body, carry)
        done = done + trips * run
    return carry


def _moba_prompt_kernel(q_ref, k_ref, vt_ref, bias_ref, o_ref, means_s, t_s, *, n_blocks):
    i = pl.program_id(1)
    kblk = lambda j: k_ref[pl.ds(pl.multiple_of(j * MB_BLOCK, MB_BLOCK), MB_BLOCK), :]

    @pl.when(i == 0)
    def _():
        def body(j, carry):
            means_s[pl.ds(j, 1), :] = jnp.sum(kblk(j).astype(F32), axis=0, keepdims=True) * (1.0 / MB_BLOCK)
            return carry
        lax.fori_loop(0, n_blocks, body, 0)

    q = q_ref[...]
    c = MB_DH ** -0.5 * LOG2E
    m_hi, m_lo = _split_hi_lo(means_s[...])
    gate = _dot_nt(m_hi, q) + _dot_nt(m_lo, q)
    blk = lax.broadcasted_iota(jnp.int32, gate.shape, 0)
    gate = jnp.where(blk < i, gate, NEG)
    sel = []
    for _ in range(MB_TOPK):
        mx = jnp.max(gate, axis=0, keepdims=True)
        idx = jnp.min(jnp.where(gate == mx, blk, n_blocks), axis=0, keepdims=True)
        sel.append(jnp.where(idx < i, idx, -1))
        gate = jnp.where(blk == idx, -jnp.inf, gate)

    n_groups = (i + MOBA_GROUP) // MOBA_GROUP
    clamp = lambda jt: jnp.minimum(jt, n_blocks - 1)

    def chosen(jt):
        return (sel[0] == jt) | (sel[1] == jt) | (sel[2] == jt) | (jt == i)

    def logits_and_max(first, count, m):
        for u in range(count):
            jt = first + u
            t = _dot_nt(kblk(clamp(jt)), q) * c + bias_ref[0, jnp.clip(i - jt, 0, NEAR_BLOCKS)]
            t_s[jt] = t
            m = jnp.maximum(m, jnp.where(chosen(jt), jnp.max(t, axis=0, keepdims=True), NEG))
        return m

    m = _run_lengths(n_groups, logits_and_max, jnp.full((1, MB_BLOCK), NEG, F32))

    ones = jnp.ones((ONES_ROWS, MB_BLOCK), BF16)

    def accumulate(first, count, acc):
        for u in range(count):
            jt = first + u
            shift = jnp.where(chosen(jt), m, BIG)
            p = jnp.exp2((t_s[jt] - shift).astype(BF16))
            acc = acc + _dot(jnp.concatenate([vt_ref[clamp(jt)], ones], axis=0), p)
        return acc

    acc = _run_lengths(n_groups, accumulate, jnp.zeros((MB_DH + ONES_ROWS, MB_BLOCK), F32))
    o_ref[...] = (acc[:MB_DH] / acc[MB_DH:MB_DH + 1]).T.astype(BF16)


def _moba_prompt(aq, ak, av_t, bias_tiles):
    n = aq.shape[0]
    assert n % MB_BLOCK == 0
    nb = n // MB_BLOCK
    assert av_t.shape == (nb, MB_HEADS * MB_DH, MB_BLOCK)
    n_tiles = -(-nb // MOBA_GROUP) * MOBA_GROUP
    return pl.pallas_call(
        functools.partial(_moba_prompt_kernel, n_blocks=nb),
        grid=(MB_HEADS, nb),
        in_specs=[pl.BlockSpec((MB_BLOCK, MB_DH), lambda h, i: (i, h)),
                  pl.BlockSpec((n, MB_DH), lambda h, i: (0, h)),
                  pl.BlockSpec((nb, MB_DH, MB_BLOCK), lambda h, i: (0, h, 0)),
                  pl.BlockSpec((1, NEAR_BLOCKS + 1, MB_BLOCK, MB_BLOCK), lambda h, i: (h, 0, 0, 0))],
        out_specs=pl.BlockSpec((MB_BLOCK, MB_DH), lambda h, i: (i, h)),
        out_shape=jax.ShapeDtypeStruct((n, MB_HEADS * MB_DH), BF16),
        scratch_shapes=[pltpu.VMEM((nb, MB_DH), F32), pltpu.VMEM((n_tiles, MB_BLOCK, MB_BLOCK), F32)],
        compiler_params=_cparams("parallel", "arbitrary"),
        name="moba_prompt",
    )(aq, ak, av_t, bias_tiles)


PAGES_PER_STEP = 16
PAGE_RING = 3


def _page_ring_step(pt_ref, cache_hbm, buf, sem, *, batch, n_steps):
    lin = pl.program_id(0) * n_steps + pl.program_id(1)
    n_lin = batch * n_steps

    def copies(l):
        b, s, slot = l // n_steps, l % n_steps, l % PAGE_RING
        return [pltpu.make_async_copy(cache_hbm.at[pt_ref[b, s * PAGES_PER_STEP + r]],
                                      buf.at[slot, r], sem.at[slot, r]) for r in range(PAGES_PER_STEP)]

    def start(l):
        for r, cp in enumerate(copies(l)):
            cp.start(priority=r % 2)

    @pl.when(lin == 0)
    def _():
        for l in range(min(PAGE_RING - 1, n_lin)):
            start(jnp.int32(l))

    @pl.when(lin + PAGE_RING - 1 < n_lin)
    def _():
        start(lin + PAGE_RING - 1)

    for cp in copies(lin):
        cp.wait()
    return lin % PAGE_RING


def _page_heads_on_lanes(buf, slot, r, page):
    heads = [buf[slot, r, pl.ds(h, page, stride=MB_HEADS), :] for h in range(MB_HEADS)]
    return jnp.concatenate(heads, axis=1).astype(BF16)


def _moba_sample_scores_kernel(pt_ref, qbd_ref, k_hbm, knew_ref, bias_ref, newb_ref, p_ref, pnew_ref, l_ref,
                               t_s, gate_s, bmax_s, kbuf, ksem, *, page, past, t_new, batch):
    step = pl.program_id(1)
    n_steps = pl.num_programs(1)
    slot = _page_ring_step(pt_ref, k_hbm, kbuf, ksem, batch=batch, n_steps=past // page // PAGES_PER_STEP)
    nbp = past // MB_BLOCK
    ppb = MB_BLOCK // page
    n_far_pages = (past - N_NEAR) // page
    lanes = MB_HEADS * t_new
    scale = MB_DH ** -0.5
    qbd = qbd_ref[0]

    for b in range(PAGES_PER_STEP // ppb):
        gate, bmax = 0.0, NEG
        for r in range(b * ppb, (b + 1) * ppb):
            pg = step * PAGES_PER_STEP + r
            s = _dot(_page_heads_on_lanes(kbuf, slot, r, page), qbd)
            brow = pl.multiple_of(jnp.maximum(pg - n_far_pages + 1, 0) * page, page)
            t = s * scale + bias_ref[pl.ds(brow, page), :]
            t_s[pl.ds(pl.multiple_of(pg * page, page), page), :] = t
            gate = gate + jnp.sum(s, axis=0, keepdims=True)
            bmax = jnp.maximum(bmax, jnp.max(t, axis=0, keepdims=True))
        blk = step * (PAGES_PER_STEP // ppb) + b
        gate_s[pl.ds(blk, 1), :] = gate * (1.0 / MB_BLOCK)
        bmax_s[pl.ds(blk, 1), :] = bmax

    @pl.when(step == n_steps - 1)
    def _():
        gate = gate_s[...]
        bidx = lax.broadcasted_iota(jnp.int32, gate.shape, 0)
        chosen = jnp.zeros(gate.shape, jnp.bool_)
        for _ in range(MB_TOPK):
            mx = jnp.max(gate, axis=0, keepdims=True)
            idx = jnp.min(jnp.where(gate == mx, bidx, nbp), axis=0, keepdims=True)
            chosen = chosen | ((bidx == idx) & (mx > -jnp.inf))
            gate = jnp.where(bidx == idx, -jnp.inf, gate)

        tk = lax.broadcasted_iota(jnp.int32, (t_new, lanes), 0)
        tq = lax.broadcasted_iota(jnp.int32, (t_new, lanes), 1) % t_new
        t_new_keys = jnp.where(tk <= tq, _dot(knew_ref[0], qbd) * scale + newb_ref[...], NEG)

        m = jnp.maximum(jnp.max(t_new_keys, axis=0, keepdims=True),
                        jnp.max(jnp.where(chosen, bmax_s[...], NEG), axis=0, keepdims=True))
        e_new = jnp.exp(t_new_keys - m)
        pnew_ref[0] = e_new
        gate_s[...] = jnp.where(chosen, 0.0, NEG)

        def exp_block(j, l):
            rows = pl.ds(pl.multiple_of(j * MB_BLOCK, MB_BLOCK), MB_BLOCK)
            e = jnp.exp(jnp.minimum(t_s[rows, :] - m, 0.0) + gate_s[pl.ds(j, 1), :])
            p_ref[0, rows, :] = e.astype(BF16)
            return l + jnp.sum(e, axis=0, keepdims=True)

        l_ref[0] = lax.fori_loop(0, nbp, exp_block, jnp.sum(e_new, axis=0, keepdims=True))


def _moba_sample_values_kernel(pt_ref, p_ref, v_hbm, pnew_ref, vnew_ref, l_ref, o_ref, acc_s, vbuf, vsem,
                               *, page, t_new, batch, n_steps):
    step = pl.program_id(1)
    slot = _page_ring_step(pt_ref, v_hbm, vbuf, vsem, batch=batch, n_steps=n_steps)

    @pl.when(step == 0)
    def _():
        acc_s[...] = jnp.zeros_like(acc_s)

    for r in range(PAGES_PER_STEP):
        acc_s[...] += _dot_tn(p_ref[0, r * page:(r + 1) * page, :], _page_heads_on_lanes(vbuf, slot, r, page))

    @pl.when(step == pl.num_programs(1) - 1)
    def _():
        acc = acc_s[...] + _dot_tn(pnew_ref[0].astype(BF16), vnew_ref[0])
        acc = acc / l_ref[0]
        parts = [acc[h * t_new:(h + 1) * t_new, h * MB_DH:(h + 1) * MB_DH] for h in range(MB_HEADS)]
        o_ref[0] = jnp.concatenate(parts, axis=1).astype(BF16)


def _moba_sample(aq, ak16, av16, cache_k, cache_v, page_table, cached_b, new_b, batch):
    t_new = aq.shape[0] // batch
    n_pool, page = cache_k.shape[:2]
    n_pages = page_table.shape[1]
    past = n_pages * page
    lanes = MB_HEADS * t_new
    assert MB_BLOCK % page == 0 and PAGES_PER_STEP % (MB_BLOCK // page) == 0
    assert n_pages % PAGES_PER_STEP == 0 and past % MB_BLOCK == 0 and past >= N_NEAR
    n_steps = n_pages // PAGES_PER_STEP
    ck = cache_k.reshape(n_pool, page * MB_HEADS, MB_DH)
    cv = cache_v.reshape(n_pool, page * MB_HEADS, MB_DH)
    q4 = aq.reshape(batch, t_new, MB_HEADS, MB_DH)
    qbd = jnp.einsum('bthd,hg->bhdgt', q4, jnp.eye(MB_HEADS, dtype=aq.dtype)).reshape(
        batch, MB_HEADS * MB_DH, lanes)

    per_batch = lambda shape: pl.BlockSpec((1,) + shape, lambda b, s, pt: (b,) + (0,) * len(shape))
    const = lambda shape: pl.BlockSpec(shape, lambda b, s, pt: (0,) * len(shape))
    in_hbm = pl.BlockSpec(memory_space=pl.ANY)
    ring_scratch = [pltpu.VMEM((PAGE_RING, PAGES_PER_STEP, page * MB_HEADS, MB_DH), F32),
                    pltpu.SemaphoreType.DMA((PAGE_RING, PAGES_PER_STEP))]

    p, p_new, l = pl.pallas_call(
        functools.partial(_moba_sample_scores_kernel, page=page, past=past, t_new=t_new, batch=batch),
        grid_spec=pltpu.PrefetchScalarGridSpec(
            num_scalar_prefetch=1,
            grid=(batch, n_steps),
            in_specs=[per_batch((MB_HEADS * MB_DH, lanes)), in_hbm, per_batch((t_new, MB_HEADS * MB_DH)),
                      const(cached_b.shape), const(new_b.shape)],
            out_specs=[per_batch((past, lanes)), per_batch((t_new, lanes)), per_batch((1, lanes))],
            scratch_shapes=[pltpu.VMEM((past, lanes), F32), pltpu.VMEM((past // MB_BLOCK, lanes), F32),
                            pltpu.VMEM((past // MB_BLOCK, lanes), F32)] + ring_scratch),
        out_shape=[jax.ShapeDtypeStruct((batch, past, lanes), BF16),
                   jax.ShapeDtypeStruct((batch, t_new, lanes), F32),
                   jax.ShapeDtypeStruct((batch, 1, lanes), F32)],
        compiler_params=_cparams("arbitrary", "arbitrary"),
        name="moba_sample_scores",
    )(page_table, qbd, ck, ak16.reshape(batch, t_new, -1), cached_b, new_b)

    y = pl.pallas_call(
        functools.partial(_moba_sample_values_kernel, page=page, t_new=t_new, batch=batch, n_steps=n_steps),
        grid_spec=pltpu.PrefetchScalarGridSpec(
            num_scalar_prefetch=1,
            grid=(batch, n_steps),
            in_specs=[pl.BlockSpec((1, PAGES_PER_STEP * page, lanes), lambda b, s, pt: (b, s, 0)), in_hbm,
                      per_batch((t_new, lanes)), per_batch((t_new, MB_HEADS * MB_DH)), per_batch((lanes, 1))],
            out_specs=per_batch((t_new, MB_HEADS * MB_DH)),
            scratch_shapes=[pltpu.VMEM((lanes, MB_HEADS * MB_DH), F32)] + ring_scratch),
        out_shape=jax.ShapeDtypeStruct((batch, t_new, MB_HEADS * MB_DH), BF16),
        compiler_params=_cparams("arbitrary", "arbitrary"),
        name="moba_sample_values",
    )(page_table, p, cv, p_new, av16.reshape(batch, t_new, -1), l.reshape(batch, lanes, 1))
    return y.reshape(batch * t_new, MB_HEADS * MB_DH)


def _memkv_kernel(mem_ref, g_ref, w_ref, kn_ref, mk_ref, mv_ref):
    xb = _rms(mem_ref[...], g_ref[...]).astype(BF16)
    width = X_HEADS * X_DH
    mk_ref[...] = _head_rms(_dot(xb, w_ref[:, :width]), kn_ref[...], X_DH)
    mv_ref[...] = _dot(xb, w_ref[:, width:])


def _memkv(mem, mem_norm_g, w_mem_kv, cross_kn_g):
    m = mem.shape[0]
    width = X_HEADS * X_DH
    return pl.pallas_call(
        _memkv_kernel,
        out_shape=[jax.ShapeDtypeStruct((m, width), F32), jax.ShapeDtypeStruct((m, width), F32)],
        compiler_params=pltpu.CompilerParams(vmem_limit_bytes=VMEM_LIMIT),
        name="memkv",
    )(mem, mem_norm_g, w_mem_kv, cross_kn_g)


def _cross_kernel(q_ref, mk_ref, mv_ref, o_ref, *, head_major_rows):
    scale = X_DH ** -0.5
    lane_tiles = X_DH // 128

    def head(ref, h):
        if not head_major_rows:
            return ref[0, :, h * X_DH:(h + 1) * X_DH].astype(BF16)
        n_mem = ref.shape[1] // (X_HEADS * lane_tiles)
        parts = [ref[0, pl.ds(h * lane_tiles + c, n_mem, stride=X_HEADS * lane_tiles), :]
                 for c in range(lane_tiles)]
        return jnp.concatenate(parts, axis=1).astype(BF16)

    for h in range(X_HEADS):
        cols = slice(h * X_DH, (h + 1) * X_DH)
        s = _dot_nt(q_ref[0, :, cols], head(mk_ref, h)) * scale
        p = jnp.exp(s - jnp.max(s, axis=1, keepdims=True))
        p = p / jnp.sum(p, axis=1, keepdims=True)
        o_ref[0, :, cols] = _dot(p.astype(BF16), head(mv_ref, h)).astype(BF16)


def _cross(cq, mk, mv, batch, tq):
    n = cq.shape[0]
    t = n // batch
    assert t % tq == 0
    nt = t // tq
    width = X_HEADS * X_DH
    head_major_rows = mk.ndim == 4
    if head_major_rows:
        mk, mv = (a.reshape(batch, -1, 128) for a in (mk, mv))
    q_spec = pl.BlockSpec((1, tq, width), lambda b, i: (b * nt + i, 0, 0))
    mem_spec = pl.BlockSpec((1,) + mk.shape[1:], lambda b, i: (b, 0, 0))
    y = pl.pallas_call(
        functools.partial(_cross_kernel, head_major_rows=head_major_rows),
        grid=(batch, nt),
        in_specs=[q_spec, mem_spec, mem_spec],
        out_specs=q_spec,
        out_shape=jax.ShapeDtypeStruct((batch * nt, tq, width), BF16),
        compiler_params=_cparams("parallel", "arbitrary"),
        name="cross",
    )(cq.reshape(batch * nt, tq, width), mk, mv)
    return y.reshape(n, width)


def _merge_kernel(x_ref, ym_ref, ya_ref, yc_ref, g_ref, wm_ref, wa_ref, wc_ref, wo_ref, o_ref):
    mixed = (g_ref[:, 0:1024] * _dot(ym_ref[...], wm_ref[...])
             + g_ref[:, 1024:2048] * _dot(ya_ref[...], wa_ref[...])
             + g_ref[:, 2048:3072] * _dot(yc_ref[...], wc_ref[...]))
    o_ref[...] = x_ref[...] + _dot(mixed.astype(BF16), wo_ref[...])


def _merge(x, y_m, y_a, y_c, g, w_m, w_a, w_c, w_o, tm=512):
    n = x.shape[0]
    tm = min(tm, n)
    assert n % tm == 0
    row = lambda w: pl.BlockSpec((tm, w), lambda i: (i, 0))
    return pl.pallas_call(
        _merge_kernel,
        grid=(n // tm,),
        in_specs=[row(1024), row(1024), row(1024), row(1024), row(3072)]
                 + [_resident((1024, 1024))] * 4,
        out_specs=row(1024),
        out_shape=jax.ShapeDtypeStruct((n, 1024), F32),
        compiler_params=_cparams("parallel"),
        name="merge",
    )(x, y_m, y_a, y_c, g, w_m, w_a, w_c, w_o)


def _swiglu_kernel(x_ref, g_ref, wgu_ref, wd_ref, o_ref):
    x = x_ref[...]
    gu = _dot(_rms(x, g_ref[...]).astype(BF16), wgu_ref[...])
    d_ff = wd_ref.shape[0]
    gt, up = gu[:, :d_ff], gu[:, d_ff:]
    o_ref[...] = x + _dot((gt * jax.nn.sigmoid(gt) * up).astype(BF16), wd_ref[...])


def _swiglu(x, norm2_g, w_gu, w_down, tm=512):
    n = x.shape[0]
    tm = min(tm, n)
    assert n % tm == 0
    row = pl.BlockSpec((tm, 1024), lambda i: (i, 0))
    return pl.pallas_call(
        _swiglu_kernel,
        grid=(n // tm,),
        in_specs=[row, _resident((1, 1024)), _resident(w_gu.shape), _resident(w_down.shape)],
        out_specs=row,
        out_shape=jax.ShapeDtypeStruct((n, 1024), F32),
        compiler_params=_cparams("parallel"),
        name="swiglu",
    )(x, norm2_g, w_gu, w_down)


def _pick_chunk(t):
    for c in (512, 256, 128, 64):
        if t % c == 0:
            return c
    return t


def _layer(x3, ml_state, mem_k, mem_v, moba, w):
    batch, t, _ = x3.shape
    x = x3.reshape(batch * t, D_MODEL)
    (qk, v, o_sig, aq, ak32, ak16, av32, av16, av_t, cq, g, gif) = _inproj(
        x, w['norm1_g'], w['w_main'], w['w_if'], w['b_if'], w['moba_qn_g'], w['moba_kn_g'], w['cross_qn_g'])
    y_m, c_out, n_out, m_out = _mlstm(qk, v, o_sig, gif, ml_state[0], ml_state[1], ml_state[2],
                                      w['mlstm_hn_g'], batch, _pick_chunk(t))
    y_a = moba(aq, ak16, av16, av_t)
    y_c = _cross(cq, mem_k, mem_v, batch, min(t, 1024))
    x1 = _merge(x, y_m, y_a, y_c, g, w['w_br_m'], w['w_br_a'], w['w_br_c'], w['w_out'])
    y = _swiglu(x1, w['norm2_g'], w['w_gu'], w['w_down'])
    return (y.reshape(batch, t, D_MODEL), ak32.reshape(batch, t, MB_HEADS, MB_DH),
            av32.reshape(batch, t, MB_HEADS, MB_DH), c_out, n_out, m_out.reshape(batch, ML_HEADS))


def kernel(x_prompt, x_sample, mem_prompt, cache_k, cache_v, cache_mem_k, cache_mem_v, state_C, state_n, state_m, page_table, norm1_g, w_in, b_ig, b_fg, mlstm_hn_g, moba_qn_g, moba_kn_g, rel_bias, cross_qn_g, cross_kn_g, mem_norm_g, w_mem_kv, w_br_m, w_br_a, w_br_c, w_out, norm2_g, w_gu, w_down):
    c_if = 2 * ML_HEADS * ML_DQK + 2 * ML_HEADS * ML_DV
    row = lambda a: a.reshape(1, -1)
    w = {
        'norm1_g': row(norm1_g),
        'w_main': jnp.concatenate([w_in[:, :c_if], w_in[:, c_if + 2 * ML_HEADS:]], axis=1).astype(BF16),
        'w_if': w_in[:, c_if:c_if + 2 * ML_HEADS],
        'b_if': row(jnp.concatenate([b_ig, b_fg])),
        'mlstm_hn_g': row(mlstm_hn_g), 'moba_qn_g': row(moba_qn_g), 'moba_kn_g': row(moba_kn_g),
        'cross_qn_g': row(cross_qn_g),
        'w_br_m': w_br_m.astype(BF16), 'w_br_a': w_br_a.astype(BF16), 'w_br_c': w_br_c.astype(BF16),
        'w_out': w_out.astype(BF16), 'norm2_g': row(norm2_g),
        'w_gu': w_gu.astype(BF16), 'w_down': w_down.astype(BF16),
    }
    assert w['w_main'].shape[1] == N_SEG * 1024

    bp, sp, _ = x_prompt.shape
    assert bp == 1
    zero_state = (jnp.zeros((bp, ML_HEADS, ML_DQK, ML_DV), F32), jnp.zeros((bp, ML_HEADS, ML_DQK), F32),
                  jnp.zeros((bp, ML_HEADS), F32))
    n_mem = mem_prompt.shape[1]
    mem_k_p, mem_v_p = _memkv(mem_prompt.reshape(bp * n_mem, D_MODEL), row(mem_norm_g),
                              w_mem_kv.astype(BF16), row(cross_kn_g))
    mem_k_p = mem_k_p.reshape(bp, n_mem, X_HEADS * X_DH)
    mem_v_p = mem_v_p.reshape(bp, n_mem, X_HEADS * X_DH)
    bias_tiles = _prompt_bias(rel_bias)
    y_p, k_p, v_p, c_p, n_p, m_p = _layer(
        x_prompt, zero_state, mem_k_p, mem_v_p,
        lambda aq, ak, av, av_t: _moba_prompt(aq, ak, av_t, bias_tiles), w)

    bs, ts, _ = x_sample.shape
    rb_lanes = jnp.repeat(rel_bias, ts, axis=1)
    cached_b, new_b = _sample_bias(rb_lanes, cache_k.shape[1], ts)
    y_s, k_s, v_s, c_s, n_s, m_s = _layer(
        x_sample, (state_C, state_n, state_m),
        cache_mem_k, cache_mem_v,
        lambda aq, ak, av, av_t: _moba_sample(aq, ak, av, cache_k, cache_v, page_table, cached_b, new_b, bs), w)

    return (y_p, y_s, k_p, v_p, mem_k_p.reshape(bp, n_mem, X_HEADS, X_DH),
            mem_v_p.reshape(bp, n_mem, X_HEADS, X_DH), c_p, n_p, m_p, k_s, v_s, c_s, n_s, m_s)
```

```python
import functools
import math

import jax
import jax.numpy as jnp
from jax import lax
from jax.experimental import pallas as pl
from jax.experimental.pallas import tpu as pltpu

F32 = jnp.float32
BF16 = jnp.bfloat16

EPS = 1e-6
D_MODEL = 1024
ML_HEADS, ML_DQK, ML_DV = 4, 128, 256
GATE_CAP = 15.0
MB_HEADS, MB_DH, MB_BLOCK, MB_TOPK = 8, 128, 256, 3
X_HEADS, X_DH = 4, 256
REL_BUCKETS, REL_MAX_DIST = 32, 1024
NEG = -1e30
BIG = 1e30
LOG2E = math.log2(math.e)
NEAR_BLOCKS = 5
N_SEG = 10

VMEM_LIMIT = 56 * 1024 * 1024


def _cparams(*sem):
    return pltpu.CompilerParams(dimension_semantics=sem, vmem_limit_bytes=VMEM_LIMIT)


def _resident(shape):
    nd = len(shape)
    return pl.BlockSpec(shape, lambda *_: (0,) * nd, pipeline_mode=pl.Buffered(1))


def _rms(x, g):
    return x * lax.rsqrt(jnp.mean(x * x, axis=-1, keepdims=True) + EPS) * g


def _head_rms(x, g, width):
    parts = [_rms(x[:, s:s + width], g) for s in range(0, x.shape[1], width)]
    return jnp.concatenate(parts, axis=1)


def _dot(a, b):
    return jnp.dot(a, b, preferred_element_type=F32)


def _dot_nt(a, b):
    return lax.dot_general(a, b, (((1,), (1,)), ((), ())), preferred_element_type=F32)


def _dot_tn(a, b):
    return lax.dot_general(a, b, (((0,), (0,)), ((), ())), preferred_element_type=F32)


def _split_hi_lo(x):
    hi = x.astype(BF16)
    lo = (x - hi.astype(F32)).astype(BF16)
    return hi, lo


def _rel_bucket(dist):
    max_exact = REL_BUCKETS // 2
    d = jnp.maximum(dist, 1).astype(F32)
    large = max_exact + (jnp.log(d / max_exact) / math.log(REL_MAX_DIST / max_exact)
                         * (REL_BUCKETS - max_exact)).astype(jnp.int32)
    large = jnp.minimum(large, REL_BUCKETS - 1)
    return jnp.where(dist < max_exact, dist, large)


def _inproj_kernel(x_ref, g1_ref, w_ref, wif_ref, bif_ref, qn_ref, kn_ref, cqn_ref,
                   qk_ref, v_ref, o_ref, aq_ref, ak32_ref, ak16_ref, av32_ref, av16_ref, avt_ref,
                   cq_ref, g_ref, gif_ref):
    xn = _rms(x_ref[...], g1_ref[...])
    xb = xn.astype(BF16)

    def seg(s):
        return _dot(xb, w_ref[:, s * 1024:(s + 1) * 1024])

    lane = lax.broadcasted_iota(jnp.int32, (1, 1024), 1)
    kscale = jnp.where(lane >= ML_HEADS * ML_DQK, ML_DQK ** -0.5, 1.0).astype(F32)
    qk_ref[...] = (seg(0) * kscale).astype(BF16)
    v_ref[...] = seg(1).astype(BF16)
    o_ref[...] = jax.nn.sigmoid(seg(2))
    aq_ref[...] = _head_rms(seg(3), qn_ref[...], MB_DH).astype(BF16)
    ak = _head_rms(seg(4), kn_ref[...], MB_DH)
    ak32_ref[...] = ak
    ak16_ref[...] = ak.astype(BF16)
    av = seg(5)
    av32_ref[...] = av
    av16_ref[...] = av.astype(BF16)
    avt_ref[0] = av.T.astype(BF16)
    cq_ref[...] = _head_rms(seg(6), cqn_ref[...], X_DH).astype(BF16)
    for s in range(3):
        g_ref[:, s * 1024:(s + 1) * 1024] = jax.nn.sigmoid(seg(7 + s))

    x_lo = (xn - xb.astype(F32)).astype(BF16)
    w_hi, w_lo = _split_hi_lo(wif_ref[...])
    z = _dot(xb, w_hi) + (_dot(x_lo, w_hi) + _dot(xb, w_lo)) + bif_ref[...]
    capped = GATE_CAP * jnp.tanh(z / GATE_CAP)
    log_sig = -(jnp.maximum(-capped, 0.0) + jnp.log1p(jnp.exp(-jnp.abs(capped))))
    col = lax.broadcasted_iota(jnp.int32, z.shape, 1)
    gif_ref[...] = jnp.where(col < ML_HEADS, capped, log_sig)


def _inproj(x, norm1_g, w_main, w_if, b_if, qn_g, kn_g, cqn_g, tm=256):
    n = x.shape[0]
    tm = min(tm, n)
    assert n % tm == 0
    row = lambda w: pl.BlockSpec((tm, w), lambda i: (i, 0))
    shapes = [(1024, BF16), (1024, BF16), (1024, F32), (1024, BF16), (1024, F32), (1024, BF16),
              (1024, F32), (1024, BF16), None, (1024, BF16), (3072, F32), (2 * ML_HEADS, F32)]
    out_specs = [pl.BlockSpec((1, 1024, tm), lambda i: (i, 0, 0)) if s is None else row(s[0]) for s in shapes]
    out_shape = [jax.ShapeDtypeStruct((n // tm, 1024, tm), BF16) if s is None
                 else jax.ShapeDtypeStruct((n, s[0]), s[1]) for s in shapes]
    return pl.pallas_call(
        _inproj_kernel,
        grid=(n // tm,),
        in_specs=[row(1024), _resident((1, 1024)), _resident(w_main.shape), _resident(w_if.shape),
                  _resident((1, 2 * ML_HEADS)), _resident((1, MB_DH)), _resident((1, MB_DH)),
                  _resident((1, X_DH))],
        out_specs=out_specs,
        out_shape=out_shape,
        compiler_params=_cparams("parallel"),
        name="inproj",
    )(x, norm1_g, w_main, w_if, b_if, qn_g, kn_g, cqn_g)


def _mlstm_kernel(qk_ref, v_ref, o_ref, gc_ref, gr_ref, c0_ref, n0_ref, m0_ref, hn_ref,
                  y_ref, cout_ref, nout_ref, mout_ref, c_s, n_s, m_s, *, chunk, n_chunks):
    L = chunk
    c = pl.program_id(1)

    @pl.when(c == 0)
    def _():
        c_s[...] = c0_ref[0]
        n_s[...] = n0_ref[0]
        m_s[...] = m0_ref[0]

    gc = gc_ref[0]
    gr = gr_ref[0]
    t_idx = lax.broadcasted_iota(jnp.int32, (L, L), 0)
    s_idx = lax.broadcasted_iota(jnp.int32, (L, L), 1)
    causal = s_idx <= t_idx

    for h in range(ML_HEADS):
        ig_row, lf_row = gr[h:h + 1, :], gr[ML_HEADS + h:ML_HEADS + h + 1, :]
        ig_col, lf_col = gc[:, h:h + 1], gc[:, ML_HEADS + h:ML_HEADS + h + 1]
        b_col = jnp.sum(jnp.where(causal, lf_row, 0.0), axis=1, keepdims=True)
        b_row = jnp.sum(jnp.where(t_idx <= s_idx, lf_col, 0.0), axis=0, keepdims=True)
        m_prev = m_s[:, h:h + 1]

        log_d = jnp.where(causal, b_col - b_row + ig_row, -jnp.inf)
        m_inter = b_col + m_prev
        m_t = jnp.maximum(m_inter, jnp.max(log_d, axis=1, keepdims=True))
        w_intra = jnp.exp(log_d - m_t)
        w_inter = jnp.exp(m_inter - m_t)

        q = qk_ref[0, :, h * ML_DQK:(h + 1) * ML_DQK]
        k = qk_ref[0, :, (ML_HEADS + h) * ML_DQK:(ML_HEADS + h + 1) * ML_DQK]
        v = v_ref[0, :, h * ML_DV:(h + 1) * ML_DV]
        c_h = c_s[h]
        n_h = n_s[h:h + 1, :]

        s = _dot_nt(q, k) * w_intra
        num = w_inter * _dot(q, c_h.astype(BF16)) + _dot(s.astype(BF16), v)
        qn = jnp.sum(q.astype(F32) * n_h, axis=1, keepdims=True)
        den = w_inter * qn + jnp.sum(s, axis=1, keepdims=True)
        hh = num / jnp.maximum(jnp.abs(den), jnp.exp(-m_t))

        b_last = b_col[L - 1:L, :]
        m_new = jnp.maximum(b_last + m_prev,
                            jnp.max(b_last - b_row + ig_row, axis=1, keepdims=True))
        w_col = jnp.exp(b_last - b_col + ig_col - m_new)
        decay = jnp.exp(b_last + m_prev - m_new)
        kw = k.astype(F32) * w_col
        c_s[h] = decay * c_h + _dot_tn(kw.astype(BF16), v)
        n_s[h:h + 1, :] = decay * n_h + jnp.sum(kw, axis=0, keepdims=True)
        m_s[:, h:h + 1] = m_new

        hn = _rms(hh, hn_ref[:, h * ML_DV:(h + 1) * ML_DV])
        y_ref[0, :, h * ML_DV:(h + 1) * ML_DV] = (o_ref[0, :, h * ML_DV:(h + 1) * ML_DV] * hn).astype(BF16)

    @pl.when(c == n_chunks - 1)
    def _():
        cout_ref[0] = c_s[...]
        nout_ref[0] = n_s[...]
        mout_ref[0] = m_s[...]


def _mlstm(qk, v, o_sig, gif, c0, n0, m0, hn_g, batch, chunk):
    n = qk.shape[0]
    t = n // batch
    nc = t // chunk
    chunked = lambda a: a.reshape(batch * nc, chunk, a.shape[-1])
    gc = chunked(gif)
    gr = gc.transpose(0, 2, 1)
    row = lambda w: pl.BlockSpec((1, chunk, w), lambda b, c: (b * nc + c, 0, 0))
    state = lambda shape: pl.BlockSpec((1,) + shape, lambda b, c: (b,) + (0,) * len(shape))
    y, c_out, n_out, m_out = pl.pallas_call(
        functools.partial(_mlstm_kernel, chunk=chunk, n_chunks=nc),
        grid=(batch, nc),
        in_specs=[row(1024), row(1024), row(1024), row(2 * ML_HEADS),
                  pl.BlockSpec((1, 2 * ML_HEADS, chunk), lambda b, c: (b * nc + c, 0, 0)),
                  state((ML_HEADS, ML_DQK, ML_DV)), state((ML_HEADS, ML_DQK)), state((1, ML_HEADS)),
                  _resident((1, ML_HEADS * ML_DV))],
        out_specs=[row(1024), state((ML_HEADS, ML_DQK, ML_DV)), state((ML_HEADS, ML_DQK)),
                   state((1, ML_HEADS))],
        out_shape=[jax.ShapeDtypeStruct((batch * nc, chunk, 1024), BF16),
                   jax.ShapeDtypeStruct((batch, ML_HEADS, ML_DQK, ML_DV), F32),
                   jax.ShapeDtypeStruct((batch, ML_HEADS, ML_DQK), F32),
                   jax.ShapeDtypeStruct((batch, 1, ML_HEADS), F32)],
        scratch_shapes=[pltpu.VMEM((ML_HEADS, ML_DQK, ML_DV), F32), pltpu.VMEM((ML_HEADS, ML_DQK), F32),
                        pltpu.VMEM((1, ML_HEADS), F32)],
        compiler_params=_cparams("parallel", "arbitrary"),
        name="mlstm",
    )(chunked(qk), chunked(v), chunked(o_sig), gc, gr, c0, n0, m0.reshape(batch, 1, ML_HEADS), hn_g)
    return y.reshape(n, 1024), c_out, n_out, m_out


def _bucket_lookup(bucket, table_row):
    val = jnp.zeros(bucket.shape, F32)
    for b in range(REL_BUCKETS):
        val = jnp.where(bucket == b, table_row(b), val)
    return val


def _prompt_bias_kernel(rb_ref, o_ref):
    h = pl.program_id(0)
    kj = lax.broadcasted_iota(jnp.int32, (MB_BLOCK, MB_BLOCK), 0)
    qi = lax.broadcasted_iota(jnp.int32, (MB_BLOCK, MB_BLOCK), 1)
    for d in range(NEAR_BLOCKS + 1):
        bucket = _rel_bucket(jnp.maximum(d * MB_BLOCK + qi - kj, 0))
        bias = _bucket_lookup(bucket, lambda b: rb_ref[b, h]) * LOG2E
        o_ref[0, d] = jnp.where(kj <= qi, bias, NEG) if d == 0 else bias


def _prompt_bias(rel_bias):
    return pl.pallas_call(
        _prompt_bias_kernel,
        grid=(MB_HEADS,),
        in_specs=[pl.BlockSpec(memory_space=pltpu.SMEM)],
        out_specs=pl.BlockSpec((1, NEAR_BLOCKS + 1, MB_BLOCK, MB_BLOCK), lambda h: (h, 0, 0, 0)),
        out_shape=jax.ShapeDtypeStruct((MB_HEADS, NEAR_BLOCKS + 1, MB_BLOCK, MB_BLOCK), F32),
        compiler_params=_cparams("parallel"),
        name="prompt_bias",
    )(rel_bias)


N_NEAR = (NEAR_BLOCKS - 1) * MB_BLOCK


def _sample_bias_kernel(rbl_ref, cached_ref, new_ref, *, page, t_new):
    lanes = cached_ref.shape[1]
    lookup = lambda bucket: _bucket_lookup(bucket, lambda b: rbl_ref[b:b + 1, :])
    r = lax.broadcasted_iota(jnp.int32, cached_ref.shape, 0)
    tq = lax.broadcasted_iota(jnp.int32, cached_ref.shape, 1) % t_new
    cached_ref[...] = lookup(_rel_bucket(jnp.where(r < page, N_NEAR + 1, tq + N_NEAR + page - r)))
    r = lax.broadcasted_iota(jnp.int32, (t_new, lanes), 0)
    tq = lax.broadcasted_iota(jnp.int32, (t_new, lanes), 1) % t_new
    new_ref[...] = lookup(_rel_bucket(jnp.maximum(tq - r, 0)))


def _sample_bias(rb_lanes, page, t_new):
    lanes = rb_lanes.shape[1]
    return pl.pallas_call(
        functools.partial(_sample_bias_kernel, page=page, t_new=t_new),
        out_shape=[jax.ShapeDtypeStruct((page + N_NEAR, lanes), F32), jax.ShapeDtypeStruct((t_new, lanes), F32)],
        name="sample_bias",
    )(rb_lanes)


MOBA_GROUP = 4
MOBA_RUNS = (8, 4, 2, 1)
ONES_ROWS = 16


def _run_lengths(n_groups, fn, carry):
    done = 0
    for idx, run in enumerate(MOBA_RUNS):
        trips = (n_groups - done) // run if idx == 0 else ((n_groups - done) // run) % 2
        first = done

        def body(r, c, first=first, run=run):
            return fn((first + r * run) * MOBA_GROUP, run * MOBA_GROUP, c)
        carry = lax.fori_loop(0, trips, body, carry)
        done = done + trips * run
    return carry


GATE_UNROLL = 4
SUBLANES = 8


def _moba_prompt_kernel(q_ref, k_ref, vt_ref, bias_ref, o_ref, means_s, sel_s, t_s, *, n_blocks):
    i = pl.program_id(1)
    rows = lambda j: pl.ds(pl.multiple_of(j * MB_BLOCK, MB_BLOCK), MB_BLOCK)
    kblk = lambda j: k_ref[rows(j), :]

    @pl.when(i == 0)
    def _():
        def mean_body(j, carry):
            means_s[pl.ds(j, 1), :] = jnp.sum(kblk(j).astype(F32), axis=0, keepdims=True) * (1.0 / MB_BLOCK)
            return carry
        lax.fori_loop(0, n_blocks, mean_body, 0)

        m_hi, m_lo = _split_hi_lo(means_s[...])
        blk = lax.broadcasted_iota(jnp.int32, (n_blocks, MB_BLOCK), 0)

        def gate_body(g, carry):
            for u in range(GATE_UNROLL):
                qb = jnp.minimum(g * GATE_UNROLL + u, n_blocks - 1)
                qq = q_ref[rows(qb), :]
                gate = jnp.where(blk < qb, _dot_nt(m_hi, qq) + _dot_nt(m_lo, qq), NEG)
                for r in range(MB_TOPK):
                    mx = jnp.max(gate, axis=0, keepdims=True)
                    idx = jnp.min(jnp.where(gate == mx, blk, n_blocks), axis=0, keepdims=True)
                    sel_s[qb, r:r + 1, :] = jnp.where(idx < qb, idx, -1)
                    gate = jnp.where(blk == idx, -jnp.inf, gate)
            return carry
        lax.fori_loop(0, -(-n_blocks // GATE_UNROLL), gate_body, 0)

    q = q_ref[rows(i), :]
    c = MB_DH ** -0.5 * LOG2E
    sel = [sel_s[i, r:r + 1, :] for r in range(MB_TOPK)]

    n_groups = (i + MOBA_GROUP) // MOBA_GROUP
    clamp = lambda jt: jnp.minimum(jt, n_blocks - 1)

    def chosen(jt):
        return (sel[0] == jt) | (sel[1] == jt) | (sel[2] == jt) | (jt == i)

    def logits_and_max(first, count, m):
        for u in range(count):
            jt = first + u
            t = _dot_nt(kblk(clamp(jt)), q) * c + bias_ref[0, jnp.clip(i - jt, 0, NEAR_BLOCKS)]
            t_s[jt] = t
            m = jnp.maximum(m, jnp.where(chosen(jt), jnp.max(t, axis=0, keepdims=True), NEG))
        return m

    m = _run_lengths(n_groups, logits_and_max, jnp.full((1, MB_BLOCK), NEG, F32))

    ones = jnp.ones((ONES_ROWS, MB_BLOCK), BF16)

    def accumulate(first, count, acc):
        for u in range(count):
            jt = first + u
            shift = jnp.where(chosen(jt), m, BIG)
            p = jnp.exp2((t_s[jt] - shift).astype(BF16))
            acc = acc + _dot(jnp.concatenate([vt_ref[clamp(jt)], ones], axis=0), p)
        return acc

    acc = _run_lengths(n_groups, accumulate, jnp.zeros((MB_DH + ONES_ROWS, MB_BLOCK), F32))
    o_ref[...] = (acc[:MB_DH] / acc[MB_DH:MB_DH + 1]).T.astype(BF16)


def _moba_prompt(aq, ak, av_t, bias_tiles):
    n = aq.shape[0]
    assert n % MB_BLOCK == 0
    nb = n // MB_BLOCK
    assert av_t.shape == (nb, MB_HEADS * MB_DH, MB_BLOCK)
    n_tiles = -(-nb // MOBA_GROUP) * MOBA_GROUP
    return pl.pallas_call(
        functools.partial(_moba_prompt_kernel, n_blocks=nb),
        grid=(MB_HEADS, nb),
        in_specs=[pl.BlockSpec((n, MB_DH), lambda h, i: (0, h)),
                  pl.BlockSpec((n, MB_DH), lambda h, i: (0, h)),
                  pl.BlockSpec((nb, MB_DH, MB_BLOCK), lambda h, i: (0, h, 0)),
                  pl.BlockSpec((1, NEAR_BLOCKS + 1, MB_BLOCK, MB_BLOCK), lambda h, i: (h, 0, 0, 0))],
        out_specs=pl.BlockSpec((MB_BLOCK, MB_DH), lambda h, i: (i, h)),
        out_shape=jax.ShapeDtypeStruct((n, MB_HEADS * MB_DH), BF16),
        scratch_shapes=[pltpu.VMEM((nb, MB_DH), F32), pltpu.VMEM((nb, SUBLANES, MB_BLOCK), jnp.int32),
                        pltpu.VMEM((n_tiles, MB_BLOCK, MB_BLOCK), F32)],
        compiler_params=_cparams("parallel", "arbitrary"),
        name="moba_prompt",
    )(aq, ak, av_t, bias_tiles)


PAGES_PER_STEP = 8
PAGE_RING = 3


def _page_ring_step(pt_ref, cache_hbm, buf, sem, *, batch, n_steps):
    lin = pl.program_id(0) * n_steps + pl.program_id(1)
    n_lin = batch * n_steps

    def copies(l):
        b, s, slot = l // n_steps, l % n_steps, l % PAGE_RING
        return [pltpu.make_async_copy(cache_hbm.at[pt_ref[b, s * PAGES_PER_STEP + r]],
                                      buf.at[slot, r], sem.at[slot, r]) for r in range(PAGES_PER_STEP)]

    def start(l):
        for r, cp in enumerate(copies(l)):
            cp.start(priority=r % 2)

    @pl.when(lin == 0)
    def _():
        for l in range(min(PAGE_RING - 1, n_lin)):
            start(jnp.int32(l))

    @pl.when(lin + PAGE_RING - 1 < n_lin)
    def _():
        start(lin + PAGE_RING - 1)

    for cp in copies(lin):
        cp.wait()
    return lin % PAGE_RING


def _page_heads_on_lanes(buf, slot, r, page):
    heads = [buf[slot, r, pl.ds(h, page, stride=MB_HEADS), :] for h in range(MB_HEADS)]
    return jnp.concatenate(heads, axis=1).astype(BF16)


def _moba_sample_scores_kernel(pt_ref, qbd_ref, k_hbm, knew_ref, bias_ref, newb_ref, p_ref, pnew_ref, l_ref,
                               t_s, gate_s, bmax_s, kbuf, ksem, *, page, past, t_new, batch):
    step = pl.program_id(1)
    n_steps = pl.num_programs(1)
    slot = _page_ring_step(pt_ref, k_hbm, kbuf, ksem, batch=batch, n_steps=past // page // PAGES_PER_STEP)
    nbp = past // MB_BLOCK
    ppb = MB_BLOCK // page
    n_far_pages = (past - N_NEAR) // page
    lanes = MB_HEADS * t_new
    scale = MB_DH ** -0.5
    qbd = qbd_ref[0]

    for b in range(PAGES_PER_STEP // ppb):
        gate, bmax = 0.0, NEG
        for r in range(b * ppb, (b + 1) * ppb):
            pg = step * PAGES_PER_STEP + r
            s = _dot(_page_heads_on_lanes(kbuf, slot, r, page), qbd)
            brow = pl.multiple_of(jnp.maximum(pg - n_far_pages + 1, 0) * page, page)
            t = s * scale + bias_ref[pl.ds(brow, page), :]
            t_s[pl.ds(pl.multiple_of(pg * page, page), page), :] = t
            gate = gate + jnp.sum(s, axis=0, keepdims=True)
            bmax = jnp.maximum(bmax, jnp.max(t, axis=0, keepdims=True))
        blk = step * (PAGES_PER_STEP // ppb) + b
        gate_s[pl.ds(blk, 1), :] = gate * (1.0 / MB_BLOCK)
        bmax_s[pl.ds(blk, 1), :] = bmax

    @pl.when(step == n_steps - 1)
    def _():
        gate = gate_s[...]
        bidx = lax.broadcasted_iota(jnp.int32, gate.shape, 0)
        chosen = jnp.zeros(gate.shape, jnp.bool_)
        for _ in range(MB_TOPK):
            mx = jnp.max(gate, axis=0, keepdims=True)
            idx = jnp.min(jnp.where(gate == mx, bidx, nbp), axis=0, keepdims=True)
            chosen = chosen | ((bidx == idx) & (mx > -jnp.inf))
            gate = jnp.where(bidx == idx, -jnp.inf, gate)

        tk = lax.broadcasted_iota(jnp.int32, (t_new, lanes), 0)
        tq = lax.broadcasted_iota(jnp.int32, (t_new, lanes), 1) % t_new
        t_new_keys = jnp.where(tk <= tq, _dot(knew_ref[0], qbd) * scale + newb_ref[...], NEG)

        m = jnp.maximum(jnp.max(t_new_keys, axis=0, keepdims=True),
                        jnp.max(jnp.where(chosen, bmax_s[...], NEG), axis=0, keepdims=True))
        e_new = jnp.exp(t_new_keys - m)
        pnew_ref[0] = e_new
        gate_s[...] = jnp.where(chosen, 0.0, NEG)

        def exp_block(j, l):
            rows = pl.ds(pl.multiple_of(j * MB_BLOCK, MB_BLOCK), MB_BLOCK)
            e = jnp.exp(jnp.minimum(t_s[rows, :] - m, 0.0) + gate_s[pl.ds(j, 1), :])
            p_ref[0, rows, :] = e.astype(BF16)
            return l + jnp.sum(e, axis=0, keepdims=True)

        l_ref[0] = lax.fori_loop(0, nbp, exp_block, jnp.sum(e_new, axis=0, keepdims=True))


def _moba_sample_values_kernel(pt_ref, p_ref, v_hbm, pnew_ref, vnew_ref, l_ref, o_ref, acc_s, vbuf, vsem,
                               *, page, t_new, batch, n_steps):
    step = pl.program_id(1)
    slot = _page_ring_step(pt_ref, v_hbm, vbuf, vsem, batch=batch, n_steps=n_steps)

    @pl.when(step == 0)
    def _():
        acc_s[...] = jnp.zeros_like(acc_s)

    for r in range(PAGES_PER_STEP):
        acc_s[...] += _dot_tn(p_ref[0, r * page:(r + 1) * page, :], _page_heads_on_lanes(vbuf, slot, r, page))

    @pl.when(step == pl.num_programs(1) - 1)
    def _():
        acc = acc_s[...] + _dot_tn(pnew_ref[0].astype(BF16), vnew_ref[0])
        acc = acc / l_ref[0]
        parts = [acc[h * t_new:(h + 1) * t_new, h * MB_DH:(h + 1) * MB_DH] for h in range(MB_HEADS)]
        o_ref[0] = jnp.concatenate(parts, axis=1).astype(BF16)


def _moba_sample(aq, ak16, av16, cache_k, cache_v, page_table, cached_b, new_b, batch):
    t_new = aq.shape[0] // batch
    n_pool, page = cache_k.shape[:2]
    n_pages = page_table.shape[1]
    past = n_pages * page
    lanes = MB_HEADS * t_new
    assert MB_BLOCK % page == 0 and PAGES_PER_STEP % (MB_BLOCK // page) == 0
    assert n_pages % PAGES_PER_STEP == 0 and past % MB_BLOCK == 0 and past >= N_NEAR
    n_steps = n_pages // PAGES_PER_STEP
    ck = cache_k.reshape(n_pool, page * MB_HEADS, MB_DH)
    cv = cache_v.reshape(n_pool, page * MB_HEADS, MB_DH)
    q4 = aq.reshape(batch, t_new, MB_HEADS, MB_DH)
    qbd = jnp.einsum('bthd,hg->bhdgt', q4, jnp.eye(MB_HEADS, dtype=aq.dtype)).reshape(
        batch, MB_HEADS * MB_DH, lanes)

    per_batch = lambda shape: pl.BlockSpec((1,) + shape, lambda b, s, pt: (b,) + (0,) * len(shape))
    const = lambda shape: pl.BlockSpec(shape, lambda b, s, pt: (0,) * len(shape))
    in_hbm = pl.BlockSpec(memory_space=pl.ANY)
    ring_scratch = [pltpu.VMEM((PAGE_RING, PAGES_PER_STEP, page * MB_HEADS, MB_DH), F32),
                    pltpu.SemaphoreType.DMA((PAGE_RING, PAGES_PER_STEP))]

    p, p_new, l = pl.pallas_call(
        functools.partial(_moba_sample_scores_kernel, page=page, past=past, t_new=t_new, batch=batch),
        grid_spec=pltpu.PrefetchScalarGridSpec(
            num_scalar_prefetch=1,
            grid=(batch, n_steps),
            in_specs=[per_batch((MB_HEADS * MB_DH, lanes)), in_hbm, per_batch((t_new, MB_HEADS * MB_DH)),
                      const(cached_b.shape), const(new_b.shape)],
            out_specs=[per_batch((past, lanes)), per_batch((t_new, lanes)), per_batch((1, lanes))],
            scratch_shapes=[pltpu.VMEM((past, lanes), F32), pltpu.VMEM((past // MB_BLOCK, lanes), F32),
                            pltpu.VMEM((past // MB_BLOCK, lanes), F32)] + ring_scratch),
        out_shape=[jax.ShapeDtypeStruct((batch, past, lanes), BF16),
                   jax.ShapeDtypeStruct((batch, t_new, lanes), F32),
                   jax.ShapeDtypeStruct((batch, 1, lanes), F32)],
        compiler_params=_cparams("arbitrary", "arbitrary"),
        name="moba_sample_scores",
    )(page_table, qbd, ck, ak16.reshape(batch, t_new, -1), cached_b, new_b)

    y = pl.pallas_call(
        functools.partial(_moba_sample_values_kernel, page=page, t_new=t_new, batch=batch, n_steps=n_steps),
        grid_spec=pltpu.PrefetchScalarGridSpec(
            num_scalar_prefetch=1,
            grid=(batch, n_steps),
            in_specs=[pl.BlockSpec((1, PAGES_PER_STEP * page, lanes), lambda b, s, pt: (b, s, 0)), in_hbm,
                      per_batch((t_new, lanes)), per_batch((t_new, MB_HEADS * MB_DH)), per_batch((lanes, 1))],
            out_specs=per_batch((t_new, MB_HEADS * MB_DH)),
            scratch_shapes=[pltpu.VMEM((lanes, MB_HEADS * MB_DH), F32)] + ring_scratch),
        out_shape=jax.ShapeDtypeStruct((batch, t_new, MB_HEADS * MB_DH), BF16),
        compiler_params=_cparams("arbitrary", "arbitrary"),
        name="moba_sample_values",
    )(page_table, p, cv, p_new, av16.reshape(batch, t_new, -1), l.reshape(batch, lanes, 1))
    return y.reshape(batch * t_new, MB_HEADS * MB_DH)


def _memkv_kernel(mem_ref, g_ref, w_ref, kn_ref, mk_ref, mv_ref):
    xb = _rms(mem_ref[...], g_ref[...]).astype(BF16)
    width = X_HEADS * X_DH
    mk_ref[...] = _head_rms(_dot(xb, w_ref[:, :width]), kn_ref[...], X_DH)
    mv_ref[...] = _dot(xb, w_ref[:, width:])


def _memkv(mem, mem_norm_g, w_mem_kv, cross_kn_g):
    m = mem.shape[0]
    width = X_HEADS * X_DH
    return pl.pallas_call(
        _memkv_kernel,
        out_shape=[jax.ShapeDtypeStruct((m, width), F32), jax.ShapeDtypeStruct((m, width), F32)],
        compiler_params=pltpu.CompilerParams(vmem_limit_bytes=VMEM_LIMIT),
        name="memkv",
    )(mem, mem_norm_g, w_mem_kv, cross_kn_g)


def _cross_kernel(q_ref, mk_ref, mv_ref, o_ref, *, head_major_rows):
    scale = X_DH ** -0.5
    lane_tiles = X_DH // 128

    def head(ref, h):
        if not head_major_rows:
            return ref[0, :, h * X_DH:(h + 1) * X_DH].astype(BF16)
        n_mem = ref.shape[1] // (X_HEADS * lane_tiles)
        parts = [ref[0, pl.ds(h * lane_tiles + c, n_mem, stride=X_HEADS * lane_tiles), :]
                 for c in range(lane_tiles)]
        return jnp.concatenate(parts, axis=1).astype(BF16)

    for h in range(X_HEADS):
        cols = slice(h * X_DH, (h + 1) * X_DH)
        s = _dot_nt(q_ref[0, :, cols], head(mk_ref, h)) * scale
        p = jnp.exp(s - jnp.max(s, axis=1, keepdims=True))
        p = p / jnp.sum(p, axis=1, keepdims=True)
        o_ref[0, :, cols] = _dot(p.astype(BF16), head(mv_ref, h)).astype(BF16)


def _cross(cq, mk, mv, batch, tq):
    n = cq.shape[0]
    t = n // batch
    assert t % tq == 0
    nt = t // tq
    width = X_HEADS * X_DH
    head_major_rows = mk.ndim == 4
    if head_major_rows:
        mk, mv = (a.reshape(batch, -1, 128) for a in (mk, mv))
    q_spec = pl.BlockSpec((1, tq, width), lambda b, i: (b * nt + i, 0, 0))
    mem_spec = pl.BlockSpec((1,) + mk.shape[1:], lambda b, i: (b, 0, 0))
    y = pl.pallas_call(
        functools.partial(_cross_kernel, head_major_rows=head_major_rows),
        grid=(batch, nt),
        in_specs=[q_spec, mem_spec, mem_spec],
        out_specs=q_spec,
        out_shape=jax.ShapeDtypeStruct((batch * nt, tq, width), BF16),
        compiler_params=_cparams("parallel", "arbitrary"),
        name="cross",
    )(cq.reshape(batch * nt, tq, width), mk, mv)
    return y.reshape(n, width)


def _merge_kernel(x_ref, ym_ref, ya_ref, yc_ref, g_ref, wm_ref, wa_ref, wc_ref, wo_ref, o_ref):
    mixed = (g_ref[:, 0:1024] * _dot(ym_ref[...], wm_ref[...])
             + g_ref[:, 1024:2048] * _dot(ya_ref[...], wa_ref[...])
             + g_ref[:, 2048:3072] * _dot(yc_ref[...], wc_ref[...]))
    o_ref[...] = x_ref[...] + _dot(mixed.astype(BF16), wo_ref[...])


def _merge(x, y_m, y_a, y_c, g, w_m, w_a, w_c, w_o, tm=512):
    n = x.shape[0]
    tm = min(tm, n)
    assert n % tm == 0
    row = lambda w: pl.BlockSpec((tm, w), lambda i: (i, 0))
    return pl.pallas_call(
        _merge_kernel,
        grid=(n // tm,),
        in_specs=[row(1024), row(1024), row(1024), row(1024), row(3072)]
                 + [_resident((1024, 1024))] * 4,
        out_specs=row(1024),
        out_shape=jax.ShapeDtypeStruct((n, 1024), F32),
        compiler_params=_cparams("parallel"),
        name="merge",
    )(x, y_m, y_a, y_c, g, w_m, w_a, w_c, w_o)


def _swiglu_kernel(x_ref, g_ref, wgu_ref, wd_ref, o_ref):
    x = x_ref[...]
    gu = _dot(_rms(x, g_ref[...]).astype(BF16), wgu_ref[...])
    d_ff = wd_ref.shape[0]
    gt, up = gu[:, :d_ff], gu[:, d_ff:]
    o_ref[...] = x + _dot((gt * jax.nn.sigmoid(gt) * up).astype(BF16), wd_ref[...])


def _swiglu(x, norm2_g, w_gu, w_down, tm=512):
    n = x.shape[0]
    tm = min(tm, n)
    assert n % tm == 0
    row = pl.BlockSpec((tm, 1024), lambda i: (i, 0))
    return pl.pallas_call(
        _swiglu_kernel,
        grid=(n // tm,),
        in_specs=[row, _resident((1, 1024)), _resident(w_gu.shape), _resident(w_down.shape)],
        out_specs=row,
        out_shape=jax.ShapeDtypeStruct((n, 1024), F32),
        compiler_params=_cparams("parallel"),
        name="swiglu",
    )(x, norm2_g, w_gu, w_down)


def _pick_chunk(t):
    for c in (512, 256, 128, 64):
        if t % c == 0:
            return c
    return t


def _layer(x3, ml_state, mem_k, mem_v, moba, w):
    batch, t, _ = x3.shape
    x = x3.reshape(batch * t, D_MODEL)
    (qk, v, o_sig, aq, ak32, ak16, av32, av16, av_t, cq, g, gif) = _inproj(
        x, w['norm1_g'], w['w_main'], w['w_if'], w['b_if'], w['moba_qn_g'], w['moba_kn_g'], w['cross_qn_g'])
    y_m, c_out, n_out, m_out = _mlstm(qk, v, o_sig, gif, ml_state[0], ml_state[1], ml_state[2],
                                      w['mlstm_hn_g'], batch, _pick_chunk(t))
    y_a = moba(aq, ak16, av16, av_t)
    y_c = _cross(cq, mem_k, mem_v, batch, min(t, 1024))
    x1 = _merge(x, y_m, y_a, y_c, g, w['w_br_m'], w['w_br_a'], w['w_br_c'], w['w_out'])
    y = _swiglu(x1, w['norm2_g'], w['w_gu'], w['w_down'])
    return (y.reshape(batch, t, D_MODEL), ak32.reshape(batch, t, MB_HEADS, MB_DH),
            av32.reshape(batch, t, MB_HEADS, MB_DH), c_out, n_out, m_out.reshape(batch, ML_HEADS))


def kernel(x_prompt, x_sample, mem_prompt, cache_k, cache_v, cache_mem_k, cache_mem_v, state_C, state_n, state_m, page_table, norm1_g, w_in, b_ig, b_fg, mlstm_hn_g, moba_qn_g, moba_kn_g, rel_bias, cross_qn_g, cross_kn_g, mem_norm_g, w_mem_kv, w_br_m, w_br_a, w_br_c, w_out, norm2_g, w_gu, w_down):
    c_if = 2 * ML_HEADS * ML_DQK + 2 * ML_HEADS * ML_DV
    row = lambda a: a.reshape(1, -1)
    w = {
        'norm1_g': row(norm1_g),
        'w_main': jnp.concatenate([w_in[:, :c_if], w_in[:, c_if + 2 * ML_HEADS:]], axis=1).astype(BF16),
        'w_if': w_in[:, c_if:c_if + 2 * ML_HEADS],
        'b_if': row(jnp.concatenate([b_ig, b_fg])),
        'mlstm_hn_g': row(mlstm_hn_g), 'moba_qn_g': row(moba_qn_g), 'moba_kn_g': row(moba_kn_g),
        'cross_qn_g': row(cross_qn_g),
        'w_br_m': w_br_m.astype(BF16), 'w_br_a': w_br_a.astype(BF16), 'w_br_c': w_br_c.astype(BF16),
        'w_out': w_out.astype(BF16), 'norm2_g': row(norm2_g),
        'w_gu': w_gu.astype(BF16), 'w_down': w_down.astype(BF16),
    }
    assert w['w_main'].shape[1] == N_SEG * 1024

    bp, sp, _ = x_prompt.shape
    assert bp == 1
    zero_state = (jnp.zeros((bp, ML_HEADS, ML_DQK, ML_DV), F32), jnp.zeros((bp, ML_HEADS, ML_DQK), F32),
                  jnp.zeros((bp, ML_HEADS), F32))
    n_mem = mem_prompt.shape[1]
    mem_k_p, mem_v_p = _memkv(mem_prompt.reshape(bp * n_mem, D_MODEL), row(mem_norm_g),
                              w_mem_kv.astype(BF16), row(cross_kn_g))
    mem_k_p = mem_k_p.reshape(bp, n_mem, X_HEADS * X_DH)
    mem_v_p = mem_v_p.reshape(bp, n_mem, X_HEADS * X_DH)
    bias_tiles = _prompt_bias(rel_bias)
    y_p, k_p, v_p, c_p, n_p, m_p = _layer(
        x_prompt, zero_state, mem_k_p, mem_v_p,
        lambda aq, ak, av, av_t: _moba_prompt(aq, ak, av_t, bias_tiles), w)

    bs, ts, _ = x_sample.shape
    rb_lanes = jnp.repeat(rel_bias, ts, axis=1)
    cached_b, new_b = _sample_bias(rb_lanes, cache_k.shape[1], ts)
    y_s, k_s, v_s, c_s, n_s, m_s = _layer(
        x_sample, (state_C, state_n, state_m),
        cache_mem_k, cache_mem_v,
        lambda aq, ak, av, av_t: _moba_sample(aq, ak, av, cache_k, cache_v, page_table, cached_b, new_b, bs), w)

    return (y_p, y_s, k_p, v_p, mem_k_p.reshape(bp, n_mem, X_HEADS, X_DH),
            mem_v_p.reshape(bp, n_mem, X_HEADS, X_DH), c_p, n_p, m_p, k_s, v_s, c_s, n_s, m_s)
```
